```python
import jax, jax.numpy as jnp
from jax import lax
import numpy as np

D_MODEL = 2048
BATCH = 16
SEQ = 2048
DEPTH = 4
DEC_BATCH = 8
DEC_SEQ = 16
PAST_LEN = 1024

CHUNK = 64
RWKV_WIDTH = D_MODEL // 2
HEAD_SIZE = 64
N_HEADS = RWKV_WIDTH // HEAD_SIZE
DECAY_RANK = 64
A_RANK = 64
GATE_RANK = 160
POOL_WIDTH = D_MODEL // 2
POOL_WINDOWS = (2, 4, 8, 16)
N_POOL_GROUPS = len(POOL_WINDOWS)
POOL_GROUP = POOL_WIDTH // N_POOL_GROUPS
POOL_OUT_GROUP = D_MODEL // N_POOL_GROUPS
POOL_BUF = max(POOL_WINDOWS) - 1
D_FF = ((8 * D_MODEL // 3 + 255) // 256) * 256
SHIFT_WIDTH = 3 * RWKV_WIDTH + DECAY_RANK + A_RANK + GATE_RANK
IN_WIDTH = SHIFT_WIDTH + POOL_WIDTH + 2 * D_MODEL
N_MOD = 9
RMS_EPS = 1e-6
GN_EPS = 64e-5

kernel_name = 'rwkv7_pool_macaron_adaln_stream_step'


def rmsnorm(x, g):
    xf = x.astype(jnp.float32)
    y = xf * lax.rsqrt(jnp.mean(xf * xf, axis=-1, keepdims=True) + RMS_EPS)
    return (y * g.astype(jnp.float32)).astype(x.dtype)


def swiglu(h, w1, w3, w2):
    return (jax.nn.silu(h @ w1) * (h @ w3)) @ w2


def wkv_scan(S0, r, decay, k, v, kk, b):
    def step(S, inp):
        r_t, w_t, k_t, v_t, kk_t, b_t = inp
        sa = jnp.einsum('bhij,bhj->bhi', S, kk_t)
        S = S * w_t[:, :, None, :] - sa[..., None] * b_t[:, :, None, :] + v_t[..., None] * k_t[:, :, None, :]
        return S, jnp.einsum('bhij,bhj->bhi', S, r_t)
    xs = tuple(jnp.swapaxes(z, 0, 1) for z in (r, decay, k, v, kk, b))
    S, ys = lax.scan(step, S0, xs)
    return S, jnp.swapaxes(ys, 0, 1)


def rwkv_mixer(p, shift_prev, S0, mu, w0, w2, a0, a2, g2, k_k, k_a, r_k, lnx_w, lnx_b, w_o):
    B, T, _ = p.shape
    prev = jnp.concatenate([shift_prev[:, None, :].astype(p.dtype), p[:, :-1]], axis=1)
    xs = p + (prev - p) * mu
    o1, o2, o3 = RWKV_WIDTH, 2 * RWKV_WIDTH, 3 * RWKV_WIDTH
    o4 = o3 + DECAY_RANK
    o5 = o4 + A_RANK
    r, k, v = xs[..., :o1], xs[..., o1:o2], xs[..., o2:o3]
    wd, ad, gd = xs[..., o3:o4], xs[..., o4:o5], xs[..., o5:]
    w_log = -jax.nn.softplus(-(w0 + jnp.tanh(wd) @ w2).astype(jnp.float32)) - 0.5
    decay = jnp.exp(-jnp.exp(w_log))
    a = jax.nn.sigmoid(a0 + ad @ a2)
    g = jax.nn.sigmoid(gd) @ g2
    kk = k * k_k
    k = k * (1 + (a - 1) * k_a)
    hs = (B, T, N_HEADS, HEAD_SIZE)
    r4 = r.astype(jnp.float32).reshape(hs)
    k4 = k.astype(jnp.float32).reshape(hs)
    v4 = v.astype(jnp.float32).reshape(hs)
    a4 = a.astype(jnp.float32).reshape(hs)
    kk4 = kk.astype(jnp.float32).reshape(hs)
    kk4 = kk4 / jnp.maximum(jnp.sqrt(jnp.sum(kk4 * kk4, axis=-1, keepdims=True)), 1e-12)
    S, y = wkv_scan(S0.astype(jnp.float32), r4, decay.reshape(hs), k4, v4, kk4, kk4 * a4)
    mean = jnp.mean(y, axis=-1, keepdims=True)
    var = jnp.mean(jnp.square(y - mean), axis=-1, keepdims=True)
    y = ((y - mean) * lax.rsqrt(var + GN_EPS)).reshape(B, T, RWKV_WIDTH) * lnx_w.astype(jnp.float32) + lnx_b.astype(jnp.float32)
    bonus = jnp.sum(r4 * k4 * r_k.astype(jnp.float32), axis=-1, keepdims=True) * v4
    y = y + bonus.reshape(B, T, RWKV_WIDTH)
    out = (y.astype(p.dtype) * g) @ w_o
    return out, p[:, -1], S


def pool_mixer(xp, buf, pos0, w_pool, pool_scale):
    B, T, _ = xp.shape
    full = jnp.concatenate([buf.astype(xp.dtype), xp], axis=1).astype(jnp.float32)
    cs = jnp.cumsum(full, axis=1)
    cs = jnp.concatenate([jnp.zeros_like(cs[:, :1]), cs], axis=1)
    hi = cs[:, POOL_BUF + 1:]
    cur = full[:, POOL_BUF:]
    pos = pos0 + jnp.arange(T)
    outs = []
    for gi, w in enumerate(POOL_WINDOWS):
        sl = slice(gi * POOL_GROUP, (gi + 1) * POOL_GROUP)
        lo = cs[:, POOL_BUF + 1 - w:POOL_BUF + 1 - w + T, sl]
        cnt = jnp.minimum(w, pos + 1).astype(jnp.float32)[None, :, None]
        outs.append((hi[..., sl] - lo) / cnt - cur[..., sl])
    pooled = jnp.stack(outs, axis=2).astype(xp.dtype)
    y = jnp.einsum('btgc,gcd->btgd', pooled, w_pool).reshape(B, T, D_MODEL)
    return y * pool_scale, full[:, -POOL_BUF:].astype(xp.dtype)


def layer(x, c, wkv0, shift0, pool0, pos0, lp):
    (norm_g, w_mod, b_mod, f_w1, f_w3, f_w2, w_in, mu, w0, w_decay, a0, w_a, w_gate,
     k_k, k_a, r_k, lnx_w, lnx_b, w_o_rwkv, w_pool, pool_scale, w_out) = lp
    B = x.shape[0]
    mod = (jax.nn.silu(c) @ w_mod + b_mod).reshape(B, N_MOD, 1, D_MODEL)

    def modulate(h, i):
        return rmsnorm(h, norm_g[i]) * (1 + mod[:, 3 * i + 1]) + mod[:, 3 * i]

    h = modulate(x, 0)
    x = x + 0.5 * mod[:, 2] * swiglu(h, f_w1[0], f_w3[0], f_w2[0])
    h = modulate(x, 1)
    proj = h @ w_in
    p_rwkv = proj[..., :SHIFT_WIDTH]
    p_pool = proj[..., SHIFT_WIDTH:SHIFT_WIDTH + POOL_WIDTH]
    gates = jax.nn.sigmoid(proj[..., SHIFT_WIDTH + POOL_WIDTH:])
    o_a, shift1, wkv1 = rwkv_mixer(p_rwkv, shift0, wkv0, mu, w0, w_decay, a0, w_a, w_gate,
                                   k_k, k_a, r_k, lnx_w, lnx_b, w_o_rwkv)
    o_b, pool1 = pool_mixer(p_pool, pool0, pos0, w_pool, pool_scale)
    merged = gates[..., :D_MODEL] * o_a + gates[..., D_MODEL:] * o_b
    x = x + mod[:, 5] * (merged @ w_out)
    h = modulate(x, 2)
    x = x + 0.5 * mod[:, 8] * swiglu(h, f_w1[1], f_w3[1], f_w2[1])
    return x, wkv1, shift1, pool1


def setup_inputs(seed: int = 0) -> dict:
    key = jax.random.key(seed)
    ks = iter(jax.random.split(key, 40))

    def nrm(shape, s):
        return jax.random.normal(next(ks), shape, jnp.float32) * s

    def uni(shape, lo, hi):
        return jax.random.uniform(next(ks), shape, jnp.float32, lo, hi)

    D, R, L = D_MODEL, RWKV_WIDTH, DEPTH
    return {
        'x_prompt': nrm((BATCH, SEQ, D), 1.0),
        'x_sample': nrm((DEC_BATCH, DEC_SEQ, D), 1.0),
        'c_prompt': nrm((BATCH, D), 1.0),
        'c_sample': nrm((DEC_BATCH, D), 1.0),
        'state_wkv': nrm((L, DEC_BATCH, N_HEADS, HEAD_SIZE, HEAD_SIZE), 0.5),
        'state_shift': nrm((L, DEC_BATCH, SHIFT_WIDTH), 1.0),
        'state_pool': nrm((L, DEC_BATCH, POOL_BUF, POOL_WIDTH), 1.0),
        'norm_g': 1.0 + nrm((L, 3, D), 0.05),
        'w_mod': nrm((L, D, N_MOD * D), 0.5 * D ** -0.5),
        'b_mod': nrm((L, N_MOD * D), 0.01),
        'ffn_w1': nrm((L, 2, D, D_FF), D ** -0.5),
        'ffn_w3': nrm((L, 2, D, D_FF), D ** -0.5),
        'ffn_w2': nrm((L, 2, D_FF, D), D_FF ** -0.5),
        'w_in': nrm((L, D, IN_WIDTH), D ** -0.5),
        'shift_mu': uni((L, SHIFT_WIDTH), 0.0, 1.0),
        'w0': uni((L, R), -6.0, -1.0),
        'w_decay': nrm((L, DECAY_RANK, R), 0.1 * DECAY_RANK ** -0.5),
        'a0': nrm((L, R), 0.1),
        'w_a': nrm((L, A_RANK, R), A_RANK ** -0.5),
        'w_gate': nrm((L, GATE_RANK, R), GATE_RANK ** -0.5),
        'k_k': 0.85 + nrm((L, R), 0.05),
        'k_a': 1.0 + nrm((L, R), 0.05),
        'r_k': nrm((L, N_HEADS, HEAD_SIZE), 0.1),
        'lnx_w': 1.0 + nrm((L, R), 0.05),
        'lnx_b': nrm((L, R), 0.01),
        'w_o_rwkv': nrm((L, R, D), R ** -0.5),
        'w_pool': nrm((L, N_POOL_GROUPS, POOL_GROUP, POOL_OUT_GROUP), POOL_GROUP ** -0.5),
        'pool_scale': 1.0 + nrm((L, D), 0.05),
        'w_out': nrm((L, D, D), D ** -0.5),
        'final_g': 1.0 + nrm((D,), 0.05),
    }


def reference(x_prompt, x_sample, c_prompt, c_sample, state_wkv, state_shift, state_pool,
              norm_g, w_mod, b_mod, ffn_w1, ffn_w3, ffn_w2, w_in, shift_mu, w0, w_decay, a0, w_a,
              w_gate, k_k, k_a, r_k, lnx_w, lnx_b, w_o_rwkv, w_pool, pool_scale, w_out, final_g):
    Bp = x_prompt.shape[0]
    dt = x_prompt.dtype
    wkv_z = jnp.zeros((Bp, N_HEADS, HEAD_SIZE, HEAD_SIZE), jnp.float32)
    shift_z = jnp.zeros((Bp, SHIFT_WIDTH), dt)
    pool_z = jnp.zeros((Bp, POOL_BUF, POOL_WIDTH), dt)
    yp, ys = x_prompt, x_sample
    pw, psh, ppl, sw, ssh, spl = [], [], [], [], [], []
    for l in range(DEPTH):
        lp = (norm_g[l], w_mod[l], b_mod[l], ffn_w1[l], ffn_w3[l], ffn_w2[l], w_in[l], shift_mu[l],
              w0[l], w_decay[l], a0[l], w_a[l], w_gate[l], k_k[l], k_a[l], r_k[l], lnx_w[l], lnx_b[l],
              w_o_rwkv[l], w_pool[l], pool_scale[l], w_out[l])
        yp, s1, s2, s3 = layer(yp, c_prompt, wkv_z, shift_z, pool_z, 0, lp)
        ys, t1, t2, t3 = layer(ys, c_sample, state_wkv[l], state_shift[l], state_pool[l], PAST_LEN, lp)
        pw.append(s1.astype(dt)); psh.append(s2); ppl.append(s3)
        sw.append(t1.astype(dt)); ssh.append(t2); spl.append(t3)
    yp = rmsnorm(yp, final_g)
    ys = rmsnorm(ys, final_g)
    return (yp, ys, jnp.stack(pw), jnp.stack(psh), jnp.stack(ppl), jnp.stack(sw), jnp.stack(ssh), jnp.stack(spl))
```

```python
import functools

import jax
import jax.numpy as jnp
from jax import lax
from jax.experimental import pallas as pl
from jax.experimental.pallas import tpu as pltpu

F32 = jnp.float32
BF16 = jnp.bfloat16

D_MODEL = 2048
DEPTH = 4
RWKV_WIDTH = D_MODEL // 2
HEAD_SIZE = 64
N_HEADS = RWKV_WIDTH // HEAD_SIZE
DECAY_RANK = 64
A_RANK = 64
GATE_RANK = 160
POOL_WIDTH = D_MODEL // 2
POOL_WINDOWS = (2, 4, 8, 16)
POOL_GROUP = POOL_WIDTH // len(POOL_WINDOWS)
POOL_OUT_GROUP = D_MODEL // len(POOL_WINDOWS)
POOL_BUF = max(POOL_WINDOWS) - 1
D_FF = ((8 * D_MODEL // 3 + 255) // 256) * 256
SHIFT_WIDTH = 3 * RWKV_WIDTH + DECAY_RANK + A_RANK + GATE_RANK
N_MOD = 9
RMS_EPS = 1e-6
GN_EPS = 64e-5

LANES = 128
N_PAIRS = RWKV_WIDTH // LANES
LOWRANK = DECAY_RANK + A_RANK + GATE_RANK
LOWRANK_PAD = 3 * LANES
RWKV_COLS = 3 * RWKV_WIDTH + LOWRANK_PAD
PROJ_TILE = 512
RWKV_PAD = 7 * PROJ_TILE
POOL_CARRY = 16
VMEM_LIMIT = 56 * 1024 * 1024


def _dot(a, b):
    return jnp.dot(a, b, preferred_element_type=F32)


def _dot_nt(a, b):
    return lax.dot_general(a, b, (((1,), (1,)), ((), ())), preferred_element_type=F32)


def _dot_tn(a, b):
    return lax.dot_general(a, b, (((0,), (0,)), ((), ())), preferred_element_type=F32)


def _rows(v, tm):
    g, d = v.shape
    if g == 1:
        return v
    return jnp.broadcast_to(v[:, None, :], (g, tm // g, d)).reshape(tm, d)


def _modulated_norm(x, gain, m, sub):
    tm = x.shape[0]
    y = x * lax.rsqrt(jnp.mean(x * x, axis=-1, keepdims=True) + RMS_EPS) * gain
    return y * (1.0 + _rows(m[:, 3 * sub + 1, :], tm)) + _rows(m[:, 3 * sub, :], tm)


def _mod_kernel(c_ref, w_ref, b_ref, o_ref):
    c = c_ref[...]
    s = (c * jax.nn.sigmoid(c)).astype(BF16)
    o_ref[0] = _dot(s, w_ref[0].astype(BF16)) + b_ref[0]


def _modulation(c_all, w_mod, b_mod):
    nb = c_all.shape[0]
    width = N_MOD * D_MODEL
    tn = 1024
    return pl.pallas_call(
        _mod_kernel,
        grid=(DEPTH, width // tn),
        in_specs=[
            pl.BlockSpec((nb, D_MODEL), lambda l, j: (0, 0)),
            pl.BlockSpec((1, D_MODEL, tn), lambda l, j: (l, 0, j)),
            pl.BlockSpec((1, 1, tn), lambda l, j: (l, 0, j)),
        ],
        out_specs=pl.BlockSpec((1, nb, tn), lambda l, j: (l, 0, j)),
        out_shape=jax.ShapeDtypeStruct((DEPTH, nb, width), F32),
        compiler_params=pltpu.CompilerParams(
            dimension_semantics=("arbitrary", "arbitrary"), vmem_limit_bytes=VMEM_LIMIT),
        name="modulation",
    )(c_all, w_mod, b_mod.reshape(DEPTH, 1, width))


def _mod_spec(layer, tm, t, b_off):
    g = max(1, tm // t)
    tiles_per_stream = max(1, t // tm)
    assert b_off % g == 0 and (tm % t == 0 or t % tm == 0)
    base = b_off // g
    return pl.BlockSpec((None, g, N_MOD, D_MODEL),
                        lambda i, j: (layer, base + i // tiles_per_stream, 0, 0))


def _ffn_kernel(x_ref, m_ref, g_ref, w1_ref, w3_ref, w2_ref, fg_ref, o_ref, h_ref, acc_ref, *, sub, final):
    j = pl.program_id(1)

    @pl.when(j == 0)
    def _():
        h_ref[...] = _modulated_norm(x_ref[...], g_ref[...], m_ref[...], sub).astype(BF16)
        acc_ref[...] = jnp.zeros_like(acc_ref)

    h = h_ref[...]
    a = _dot(h, w1_ref[...])
    b = _dot(h, w3_ref[...])
    u = (a * jax.nn.sigmoid(a) * b).astype(BF16)
    acc_ref[...] += _dot(u, w2_ref[...])

    @pl.when(j == pl.num_programs(1) - 1)
    def _():
        tm = x_ref.shape[0]
        gate = _rows(m_ref[...][:, 3 * sub + 2, :], tm)
        y = x_ref[...] + (0.5 * gate) * acc_ref[...]
        if final:
            y = y * lax.rsqrt(jnp.mean(y * y, axis=-1, keepdims=True) + RMS_EPS) * fg_ref[...]
        o_ref[...] = y


def _ffn(x, mod, norm_g, w1, w3, w2, final_g, *, layer, which, sub, t, b_off, tm, final):
    n = x.shape[0]
    tf = 512
    kern = functools.partial(_ffn_kernel, sub=sub, final=final)
    return pl.pallas_call(
        kern,
        grid=(n // tm, D_FF // tf),
        in_specs=[
            pl.BlockSpec((tm, D_MODEL), lambda i, j: (i, 0)),
            _mod_spec(layer, tm, t, b_off),
            pl.BlockSpec((None, None, 1, D_MODEL), lambda i, j: (layer, sub, 0, 0)),
            pl.BlockSpec((None, None, D_MODEL, tf), lambda i, j: (layer, which, 0, j)),
            pl.BlockSpec((None, None, D_MODEL, tf), lambda i, j: (layer, which, 0, j)),
            pl.BlockSpec((None, None, tf, D_MODEL), lambda i, j: (layer, which, j, 0)),
            pl.BlockSpec((1, D_MODEL), lambda i, j: (0, 0)),
        ],
        out_specs=pl.BlockSpec((tm, D_MODEL), lambda i, j: (i, 0)),
        out_shape=jax.ShapeDtypeStruct((n, D_MODEL), F32),
        scratch_shapes=[pltpu.VMEM((tm, D_MODEL), BF16), pltpu.VMEM((tm, D_MODEL), F32)],
        compiler_params=pltpu.CompilerParams(
            dimension_semantics=("arbitrary", "arbitrary"), vmem_limit_bytes=VMEM_LIMIT),
        name="ffn",
    )(x, mod, norm_g, w1, w3, w2, final_g)


_N_RWKV_TILES = RWKV_PAD // PROJ_TILE
_N_POOL_TILES = POOL_WIDTH // PROJ_TILE
_N_GATE_TILES = 2 * D_MODEL // PROJ_TILE


def _proj_kernel(x_ref, m_ref, g_ref, w_ref, or_ref, op_ref, og_ref, h_ref):
    j = pl.program_id(1)

    @pl.when(j == 0)
    def _():
        h_ref[...] = _modulated_norm(x_ref[...], g_ref[...], m_ref[...], 1).astype(BF16)

    y = _dot(h_ref[...], w_ref[...])

    @pl.when(j < _N_RWKV_TILES)
    def _():
        or_ref[...] = y

    @pl.when(jnp.logical_and(j >= _N_RWKV_TILES, j < _N_RWKV_TILES + _N_POOL_TILES))
    def _():
        op_ref[...] = y

    @pl.when(j >= _N_RWKV_TILES + _N_POOL_TILES)
    def _():
        og_ref[...] = y


def _in_proj(x, mod, norm_g, w_in, *, layer, t, b_off, tm):
    n = x.shape[0]
    n_tiles = _N_RWKV_TILES + _N_POOL_TILES + _N_GATE_TILES
    return pl.pallas_call(
        _proj_kernel,
        grid=(n // tm, n_tiles),
        in_specs=[
            pl.BlockSpec((tm, D_MODEL), lambda i, j: (i, 0)),
            _mod_spec(layer, tm, t, b_off),
            pl.BlockSpec((None, None, 1, D_MODEL), lambda i, j: (layer, 1, 0, 0)),
            pl.BlockSpec((None, D_MODEL, PROJ_TILE), lambda i, j: (layer, 0, j)),
        ],
        out_specs=[
            pl.BlockSpec((tm, PROJ_TILE), lambda i, j: (i, jnp.minimum(j, _N_RWKV_TILES - 1))),
            pl.BlockSpec((tm, PROJ_TILE),
                         lambda i, j: (i, jnp.clip(j - _N_RWKV_TILES, 0, _N_POOL_TILES - 1))),
            pl.BlockSpec((tm, PROJ_TILE),
                         lambda i, j: (i, jnp.clip(j - _N_RWKV_TILES - _N_POOL_TILES, 0, _N_GATE_TILES - 1))),
        ],
        out_shape=[
            jax.ShapeDtypeStruct((n, RWKV_PAD), F32),
            jax.ShapeDtypeStruct((n, POOL_WIDTH), F32),
            jax.ShapeDtypeStruct((n, 2 * D_MODEL), F32),
        ],
        scratch_shapes=[pltpu.VMEM((tm, D_MODEL), BF16)],
        compiler_params=pltpu.CompilerParams(
            dimension_semantics=("arbitrary", "arbitrary"), vmem_limit_bytes=VMEM_LIMIT),
        name="in_proj",
    )(x, mod, norm_g, w_in)


def _split3(x):
    hi = x.astype(BF16)
    r1 = x - hi.astype(F32)
    mid = r1.astype(BF16)
    lo = (r1 - mid.astype(F32)).astype(BF16)
    return hi, mid, lo


def _wkv_kernel(p_ref, sh0_ref, s0_ref, mu_ref, vec_ref, wlr_ref, o_ref, sout_ref, state_ref, prev_ref, *, chunk):
    c = pl.program_id(1)
    C = chunk
    C2 = 2 * C

    @pl.when(c == 0)
    def _():
        state_ref[...] = s0_ref[0]
        prev_ref[...] = sh0_ref[0]

    p = p_ref[...]
    row = lax.broadcasted_iota(jnp.int32, (C, 1), 0)
    prev = jnp.where(row == 0, prev_ref[...], pltpu.roll(p, 1, 0))
    prev_ref[...] = p[C - 1:C, :]
    xs = p + (prev - p) * mu_ref[...]

    r = xs[:, 0:RWKV_WIDTH]
    k = xs[:, RWKV_WIDTH:2 * RWKV_WIDTH]
    v = xs[:, 2 * RWKV_WIDTH:3 * RWKV_WIDTH]
    z = xs[:, 3 * RWKV_WIDTH:RWKV_COLS]
    zc = lax.broadcasted_iota(jnp.int32, z.shape, 1)
    zact = jnp.where(zc < DECAY_RANK, jnp.tanh(z),
                     jnp.where(zc < DECAY_RANK + A_RANK, z, jax.nn.sigmoid(z)))
    lr = _dot(zact.astype(BF16), wlr_ref[...])
    w0, a0, k_k, k_a, lnx_w, lnx_b, r_k = (vec_ref[i:i + 1, :] for i in range(7))

    zz = -(w0 + lr[:, 0:RWKV_WIDTH])
    softplus = jnp.maximum(zz, 0.0) + jnp.log(1.0 + jnp.exp(-jnp.abs(zz)))
    lw = -jnp.exp(-softplus - 0.5)
    a = jax.nn.sigmoid(a0 + lr[:, RWKV_WIDTH:2 * RWKV_WIDTH])
    gate = lr[:, 2 * RWKV_WIDTH:3 * RWKV_WIDTH]

    lane = lax.broadcasted_iota(jnp.int32, (1, LANES), 1)
    head0 = lane < HEAD_SIZE
    ri = lax.broadcasted_iota(jnp.int32, (LANES, LANES), 0)
    ci = lax.broadcasted_iota(jnp.int32, (LANES, LANES), 1)
    head_ones = ((ri < HEAD_SIZE) == (ci < HEAD_SIZE)).astype(BF16)

    def head_sum(x):
        st = jnp.concatenate([x[:, g * LANES:(g + 1) * LANES] for g in range(N_PAIRS)], axis=0)
        hi, mid, _ = _split3(st)
        s = _dot(hi, head_ones) + _dot(mid, head_ones)
        return jnp.concatenate([s[g * C:(g + 1) * C] for g in range(N_PAIRS)], axis=1)

    tr = lax.broadcasted_iota(jnp.int32, (C, C), 0)
    tc = lax.broadcasted_iota(jnp.int32, (C, C), 1)
    cum_ones = (tc <= tr).astype(BF16)
    hi, mid, lo = _split3(lw)
    cs = _dot(cum_ones, hi) + _dot(cum_ones, mid) + _dot(cum_ones, lo)
    w_in = jnp.exp(cs)
    w_inv = jnp.exp(-cs)
    w_ex = jnp.exp(cs - lw)

    kk = k * k_k
    k2 = k * (1.0 + (a - 1.0) * k_a)
    kk = kk * lax.rsqrt(jnp.maximum(head_sum(kk * kk), 1e-24))
    b = kk * a
    bonus = head_sum(r * k2 * r_k) * v

    r_t = r * w_in
    kk_t = kk * w_ex
    k_h = k2 * w_inv
    b_h = b * w_inv
    w_c = w_in[C - 1:C, :]
    k_b = k_h * w_c
    b_b = b_h * w_c

    sr = lax.broadcasted_iota(jnp.int32, (C2, C2), 0)
    sc = lax.broadcasted_iota(jnp.int32, (C2, C2), 1)
    same_head = (sr < C) == (sc < C)
    strict = jnp.logical_and(same_head, sc < sr)
    incl = jnp.logical_and(same_head, sc <= sr)
    eye = (sr == sc).astype(F32)

    def stack(x):
        return jnp.concatenate([jnp.where(head0, x, 0.0), jnp.where(head0, 0.0, x)], axis=0).astype(BF16)

    ys = []
    for g in range(N_PAIRS):
        sl = slice(g * LANES, (g + 1) * LANES)
        s_prev = state_ref[g]
        q_kk = stack(kk_t[:, sl])
        q_r = stack(r_t[:, sl])
        s_kh = stack(k_h[:, sl])
        s_bh = stack(b_h[:, sl])
        s_v = stack(v[:, sl])

        a_kk_k = jnp.where(strict, _dot_nt(q_kk, s_kh), 0.0).astype(BF16)
        m_b = jnp.where(strict, _dot_nt(q_kk, s_bh), 0.0)
        a_r_k = jnp.where(incl, _dot_nt(q_r, s_kh), 0.0).astype(BF16)
        a_r_b = jnp.where(incl, _dot_nt(q_r, s_bh), 0.0).astype(BF16)

        t_inv = eye - m_b
        pw = m_b.astype(BF16)
        n = 1
        while 2 * n < C:
            pw_f = _dot(pw, pw)
            pw = pw_f.astype(BF16)
            t_inv = _dot(t_inv.astype(BF16), (eye + pw_f).astype(BF16))
            n *= 2

        s_bf = s_prev.astype(BF16)
        x1 = _dot_nt(q_kk, s_bf)
        y0 = _dot_nt(q_r, s_bf)
        rhs = _dot(a_kk_k, s_v) + x1
        sa = _dot(t_inv.astype(BF16), rhs.astype(BF16))
        sa_bf = sa.astype(BF16)
        y_st = y0 + _dot(a_r_k, s_v) - _dot(a_r_b, sa_bf)
        ys.append(y_st[:C] + y_st[C:])

        upd = _dot_tn(jnp.concatenate([s_v, sa_bf], axis=0),
                      jnp.concatenate([stack(k_b[:, sl]), -stack(b_b[:, sl])], axis=0))
        state_ref[g] = s_prev * w_c[:, sl] + upd

    y = jnp.concatenate(ys, axis=1)
    mean = head_sum(y) * (1.0 / HEAD_SIZE)
    d = y - mean
    var = head_sum(d * d) * (1.0 / HEAD_SIZE)
    yn = d * lax.rsqrt(var + GN_EPS) * lnx_w + lnx_b + bonus
    o_ref[...] = (yn * gate).astype(BF16)

    @pl.when(c == pl.num_programs(1) - 1)
    def _():
        sout_ref[0] = state_ref[...]


def _wkv(p_rwkv, shift0, s0_pairs, mu, vecs, w_lowrank, *, layer, nb, t, chunk):
    n = p_rwkv.shape[0]
    nc = t // chunk
    kern = functools.partial(_wkv_kernel, chunk=chunk)
    return pl.pallas_call(
        kern,
        grid=(nb, nc),
        in_specs=[
            pl.BlockSpec((chunk, RWKV_COLS), lambda b, c: (b * nc + c, 0)),
            pl.BlockSpec((1, 1, RWKV_COLS), lambda b, c: (b, 0, 0)),
            pl.BlockSpec((1, N_PAIRS, LANES, LANES), lambda b, c: (b, 0, 0, 0)),
            pl.BlockSpec((None, 1, RWKV_COLS), lambda b, c: (layer, 0, 0)),
            pl.BlockSpec((None, 8, RWKV_WIDTH), lambda b, c: (layer, 0, 0)),
            pl.BlockSpec((None, LOWRANK_PAD, 3 * RWKV_WIDTH), lambda b, c: (layer, 0, 0)),
        ],
        out_specs=[
            pl.BlockSpec((chunk, RWKV_WIDTH), lambda b, c: (b * nc + c, 0)),
            pl.BlockSpec((1, N_PAIRS, LANES, LANES), lambda b, c: (b, 0, 0, 0)),
        ],
        out_shape=[
            jax.ShapeDtypeStruct((n, RWKV_WIDTH), BF16),
            jax.ShapeDtypeStruct((nb, N_PAIRS, LANES, LANES), F32),
        ],
        scratch_shapes=[pltpu.VMEM((N_PAIRS, LANES, LANES), F32), pltpu.VMEM((1, RWKV_COLS), F32)],
        compiler_params=pltpu.CompilerParams(
            dimension_semantics=("arbitrary", "arbitrary"), vmem_limit_bytes=VMEM_LIMIT),
        name="wkv",
    )(p_rwkv, shift0, s0_pairs, mu, vecs, w_lowrank)


def _merge_kernel(x_ref, m_ref, yg_ref, pp_ref, buf_ref, gt_ref, wo_ref, wp_ref, ps_ref, wout_ref, o_ref, carry_ref,
                  *, pos0, tm):
    i = pl.program_id(1)

    @pl.when(i == 0)
    def _():
        carry_ref[...] = buf_ref[0]

    xp = pp_ref[...]
    full = jnp.concatenate([carry_ref[...], xp], axis=0)
    carry_ref[...] = full[tm:tm + POOL_CARRY]

    pos = pos0 + i * tm + lax.broadcasted_iota(jnp.int32, (tm, POOL_GROUP), 0)
    o_b = []
    for gi, win in enumerate(POOL_WINDOWS):
        cols = slice(gi * POOL_GROUP, (gi + 1) * POOL_GROUP)
        s = full[:, cols]
        width = 1
        while width < win:
            s = s[width:] + s[:-width]
            width *= 2
        s = s[POOL_CARRY - (win - 1):]
        cnt = jnp.minimum(win, pos + 1).astype(F32)
        pooled = s / cnt - xp[:, cols]
        o_b.append(_dot(pooled.astype(BF16), wp_ref[gi]))
    o_b = jnp.concatenate(o_b, axis=1) * ps_ref[...]
    o_a = _dot(yg_ref[...], wo_ref[...])
    gates = jax.nn.sigmoid(gt_ref[...])
    merged = gates[:, :D_MODEL] * o_a + gates[:, D_MODEL:] * o_b
    o_ref[...] = x_ref[...] + m_ref[...][:, 5, :] * _dot(merged.astype(BF16), wout_ref[...])


def _merge(x, mod, yg, p_pool, pool0, gates, w_o, w_pool, pool_scale, w_out, *, layer, nb, t, b_off, tm, pos0):
    n = x.shape[0]
    nt = t // tm
    kern = functools.partial(_merge_kernel, pos0=pos0, tm=tm)
    row = lambda b, i: (b * nt + i, 0)
    return pl.pallas_call(
        kern,
        grid=(nb, nt),
        in_specs=[
            pl.BlockSpec((tm, D_MODEL), row),
            pl.BlockSpec((None, 1, N_MOD, D_MODEL), lambda b, i: (layer, b_off + b, 0, 0)),
            pl.BlockSpec((tm, RWKV_WIDTH), row),
            pl.BlockSpec((tm, POOL_WIDTH), row),
            pl.BlockSpec((1, POOL_CARRY, POOL_WIDTH), lambda b, i: (b, 0, 0)),
            pl.BlockSpec((tm, 2 * D_MODEL), row),
            pl.BlockSpec((None, RWKV_WIDTH, D_MODEL), lambda b, i: (layer, 0, 0)),
            pl.BlockSpec((None, len(POOL_WINDOWS), POOL_GROUP, POOL_OUT_GROUP), lambda b, i: (layer, 0, 0, 0)),
            pl.BlockSpec((None, 1, D_MODEL), lambda b, i: (layer, 0, 0)),
            pl.BlockSpec((None, D_MODEL, D_MODEL), lambda b, i: (layer, 0, 0)),
        ],
        out_specs=pl.BlockSpec((tm, D_MODEL), row),
        out_shape=jax.ShapeDtypeStruct((n, D_MODEL), F32),
        scratch_shapes=[pltpu.VMEM((POOL_CARRY, POOL_WIDTH), F32)],
        compiler_params=pltpu.CompilerParams(
            dimension_semantics=("arbitrary", "arbitrary"), vmem_limit_bytes=VMEM_LIMIT),
        name="merge",
    )(x, mod, yg, p_pool, pool0, gates, w_o, w_pool, pool_scale, w_out)


def _to_pairs(s):
    b = s.shape[0]
    s = s.reshape(b, N_PAIRS, 2, HEAD_SIZE, HEAD_SIZE)
    z = jnp.zeros_like(s[:, :, 0])
    top = jnp.concatenate([s[:, :, 0], z], axis=-1)
    bot = jnp.concatenate([z, s[:, :, 1]], axis=-1)
    return jnp.concatenate([top, bot], axis=-2)


def _from_pairs(sp):
    b = sp.shape[0]
    h0 = sp[:, :, :HEAD_SIZE, :HEAD_SIZE]
    h1 = sp[:, :, HEAD_SIZE:, HEAD_SIZE:]
    return jnp.stack([h0, h1], axis=2).reshape(b, N_HEADS, HEAD_SIZE, HEAD_SIZE)


def _stream_set(x, t, nb, b_off, *, ffn_tm, proj_tm, merge_tm, chunk, pos0):
    return dict(x=x.reshape(nb * t, D_MODEL), t=t, nb=nb, b_off=b_off, ffn_tm=ffn_tm, proj_tm=proj_tm,
                merge_tm=merge_tm, chunk=chunk, pos0=pos0)


def kernel(x_prompt, x_sample, c_prompt, c_sample, state_wkv, state_shift, state_pool, norm_g, w_mod, b_mod,
           ffn_w1, ffn_w3, ffn_w2, w_in, shift_mu, w0, w_decay, a0, w_a, w_gate, k_k, k_a, r_k, lnx_w, lnx_b,
           w_o_rwkv, w_pool, pool_scale, w_out, final_g):
    bp, seq, _ = x_prompt.shape
    bs, dseq, _ = x_sample.shape
    past_len = 1024
    dt = x_prompt.dtype

    w1 = ffn_w1.astype(BF16)
    w3 = ffn_w3.astype(BF16)
    w2 = ffn_w2.astype(BF16)
    pad = jnp.zeros((DEPTH, D_MODEL, RWKV_PAD - SHIFT_WIDTH), BF16)
    w_in_b = w_in.astype(BF16)
    w_in_p = jnp.concatenate([w_in_b[:, :, :SHIFT_WIDTH], pad, w_in_b[:, :, SHIFT_WIDTH:]], axis=2)
    w_o_b = w_o_rwkv.astype(BF16)
    w_pool_b = w_pool.astype(BF16)
    w_out_b = w_out.astype(BF16)
    zr = lambda rows: jnp.zeros((DEPTH, rows, RWKV_WIDTH), F32)
    w_lr = jnp.concatenate([
        jnp.concatenate([w_decay, zr(LOWRANK_PAD - DECAY_RANK)], axis=1),
        jnp.concatenate([zr(DECAY_RANK), w_a, zr(LOWRANK_PAD - DECAY_RANK - A_RANK)], axis=1),
        jnp.concatenate([zr(DECAY_RANK + A_RANK), w_gate, zr(LOWRANK_PAD - LOWRANK)], axis=1),
    ], axis=2).astype(BF16)
    vecs = jnp.stack([w0, a0, k_k, k_a, lnx_w, lnx_b, r_k.reshape(DEPTH, RWKV_WIDTH),
                      jnp.zeros_like(w0)], axis=1)
    mu = jnp.pad(shift_mu, ((0, 0), (0, RWKV_COLS - SHIFT_WIDTH))).reshape(DEPTH, 1, RWKV_COLS)
    norm_g4 = norm_g.reshape(DEPTH, 3, 1, D_MODEL)
    fg = final_g.reshape(1, D_MODEL)
    ps = pool_scale.reshape(DEPTH, 1, D_MODEL)

    mod = _modulation(jnp.concatenate([c_prompt, c_sample], axis=0), w_mod, b_mod)
    mod = mod.reshape(DEPTH, bp + bs, N_MOD, D_MODEL)

    prompt = _stream_set(x_prompt, seq, bp, 0, ffn_tm=min(512, seq), proj_tm=min(1024, seq),
                         merge_tm=min(256, seq), chunk=min(64, seq), pos0=0)
    sample = _stream_set(x_sample, dseq, bs, bp, ffn_tm=bs * dseq, proj_tm=bs * dseq, merge_tm=dseq,
                         chunk=dseq, pos0=past_len)

    outs = {"p": ([], [], []), "s": ([], [], [])}
    for l in range(DEPTH):
        for tag, st in (("p", prompt), ("s", sample)):
            nb, t = st["nb"], st["t"]
            if tag == "p":
                shift0 = jnp.zeros((nb, 1, RWKV_COLS), F32)
                s0 = jnp.zeros((nb, N_PAIRS, LANES, LANES), F32)
                pool0 = jnp.zeros((nb, POOL_CARRY, POOL_WIDTH), F32)
            else:
                shift0 = jnp.pad(state_shift[l], ((0, 0), (0, RWKV_COLS - SHIFT_WIDTH))).reshape(nb, 1, RWKV_COLS)
                s0 = _to_pairs(state_wkv[l].astype(F32))
                pool0 = jnp.pad(state_pool[l], ((0, 0), (POOL_CARRY - POOL_BUF, 0), (0, 0)))
            common = dict(layer=l, t=t, b_off=st["b_off"])
            x = st["x"]
            x = _ffn(x, mod, norm_g4, w1, w3, w2, fg, which=0, sub=0, tm=st["ffn_tm"], final=False, **common)
            p_rwkv, p_pool, gates = _in_proj(x, mod, norm_g4, w_in_p, tm=st["proj_tm"], **common)
            yg, s_new = _wkv(p_rwkv, shift0, s0, mu, vecs, w_lr, layer=l, nb=nb, t=t, chunk=st["chunk"])
            x = _merge(x, mod, yg, p_pool, pool0, gates, w_o_b, w_pool_b, ps, w_out_b, nb=nb,
                       tm=st["merge_tm"], pos0=st["pos0"], **common)
            x = _ffn(x, mod, norm_g4, w1, w3, w2, fg, which=1, sub=2, tm=st["ffn_tm"],
                     final=(l == DEPTH - 1), **common)
            st["x"] = x
            wkv_l, shift_l, pool_l = outs[tag]
            wkv_l.append(_from_pairs(s_new).astype(dt))
            shift_l.append(p_rwkv.reshape(nb, t, RWKV_PAD)[:, -1, :SHIFT_WIDTH])
            pool_l.append(p_pool.reshape(nb, t, POOL_WIDTH)[:, t - POOL_BUF:, :])

    yp = prompt["x"].reshape(bp, seq, D_MODEL)
    ys = sample["x"].reshape(bs, dseq, D_MODEL)
    pw, psh, ppl = (jnp.stack(v) for v in outs["p"])
    sw, ssh, spl = (jnp.stack(v) for v in outs["s"])
    return (yp, ys, pw, psh, ppl, sw, ssh, spl)
```

```python
import functools

import jax
import jax.numpy as jnp
from jax import lax
from jax.experimental import pallas as pl
from jax.experimental.pallas import tpu as pltpu

F32 = jnp.float32
BF16 = jnp.bfloat16

D_MODEL = 2048
DEPTH = 4
RWKV_WIDTH = D_MODEL // 2
HEAD_SIZE = 64
N_HEADS = RWKV_WIDTH // HEAD_SIZE
DECAY_RANK = 64
A_RANK = 64
GATE_RANK = 160
POOL_WIDTH = D_MODEL // 2
POOL_WINDOWS = (2, 4, 8, 16)
POOL_GROUP = POOL_WIDTH // len(POOL_WINDOWS)
POOL_OUT_GROUP = D_MODEL // len(POOL_WINDOWS)
POOL_BUF = max(POOL_WINDOWS) - 1
D_FF = ((8 * D_MODEL // 3 + 255) // 256) * 256
SHIFT_WIDTH = 3 * RWKV_WIDTH + DECAY_RANK + A_RANK + GATE_RANK
N_MOD = 9
RMS_EPS = 1e-6
GN_EPS = 64e-5

LANES = 128
N_PAIRS = RWKV_WIDTH // LANES
LOWRANK = DECAY_RANK + A_RANK + GATE_RANK
LOWRANK_PAD = 3 * LANES
RWKV_COLS = 3 * RWKV_WIDTH + LOWRANK_PAD
PROJ_TILE = 1024
GATE_COL = 2 * D_MODEL
POOL_COL = 4 * D_MODEL
PROJ_WIDTH = POOL_COL + POOL_WIDTH
POOL_CARRY = 16
VMEM_LIMIT = 56 * 1024 * 1024


def _dot(a, b):
    return jnp.dot(a, b, preferred_element_type=F32)


def _dot_nt(a, b):
    return lax.dot_general(a, b, (((1,), (1,)), ((), ())), preferred_element_type=F32)


def _dot_tn(a, b):
    return lax.dot_general(a, b, (((0,), (0,)), ((), ())), preferred_element_type=F32)


def _rows(v, tm):
    g, d = v.shape
    if g == 1:
        return v
    return jnp.broadcast_to(v[:, None, :], (g, tm // g, d)).reshape(tm, d)


def _modulated_norm(x, gain, m, sub):
    tm = x.shape[0]
    y = x * lax.rsqrt(jnp.mean(x * x, axis=-1, keepdims=True) + RMS_EPS) * gain
    return y * (1.0 + _rows(m[:, 3 * sub + 1, :], tm)) + _rows(m[:, 3 * sub, :], tm)


def _mod_kernel(c_ref, w_ref, b_ref, o_ref):
    c = c_ref[...]
    s = (c * jax.nn.sigmoid(c)).astype(BF16)
    o_ref[0] = _dot(s, w_ref[0].astype(BF16)) + b_ref[0]


def _modulation(c_all, w_mod, b_mod):
    nb = c_all.shape[0]
    width = N_MOD * D_MODEL
    tn = 1024
    return pl.pallas_call(
        _mod_kernel,
        grid=(DEPTH, width // tn),
        in_specs=[
            pl.BlockSpec((nb, D_MODEL), lambda l, j: (0, 0)),
            pl.BlockSpec((1, D_MODEL, tn), lambda l, j: (l, 0, j)),
            pl.BlockSpec((1, 1, tn), lambda l, j: (l, 0, j)),
        ],
        out_specs=pl.BlockSpec((1, nb, tn), lambda l, j: (l, 0, j)),
        out_shape=jax.ShapeDtypeStruct((DEPTH, nb, width), F32),
        compiler_params=pltpu.CompilerParams(
            dimension_semantics=("arbitrary", "arbitrary"), vmem_limit_bytes=VMEM_LIMIT),
        name="modulation",
    )(c_all, w_mod, b_mod.reshape(DEPTH, 1, width))


def _mod_spec(layer, tm, t, b_off):
    g = max(1, tm // t)
    tiles_per_stream = max(1, t // tm)
    assert b_off % g == 0 and (tm % t == 0 or t % tm == 0)
    base = b_off // g
    return pl.BlockSpec((None, g, N_MOD, D_MODEL),
                        lambda i, j: (layer, base + i // tiles_per_stream, 0, 0))


def _ffn_kernel(x_ref, m_ref, g_ref, w1_ref, w3_ref, w2_ref, fg_ref, o_ref, h_ref, acc_ref, *, sub, final):
    j = pl.program_id(1)

    @pl.when(j == 0)
    def _():
        h_ref[...] = _modulated_norm(x_ref[...], g_ref[...], m_ref[...], sub).astype(BF16)
        acc_ref[...] = jnp.zeros_like(acc_ref)

    h = h_ref[...]
    a = _dot(h, w1_ref[...])
    b = _dot(h, w3_ref[...])
    u = (a * jax.nn.sigmoid(a) * b).astype(BF16)
    acc_ref[...] += _dot(u, w2_ref[...])

    @pl.when(j == pl.num_programs(1) - 1)
    def _():
        tm = x_ref.shape[0]
        gate = _rows(m_ref[...][:, 3 * sub + 2, :], tm)
        y = x_ref[...] + (0.5 * gate) * acc_ref[...]
        if final:
            y = y * lax.rsqrt(jnp.mean(y * y, axis=-1, keepdims=True) + RMS_EPS) * fg_ref[...]
        o_ref[...] = y


def _ffn(x, mod, norm_g, w1, w3, w2, final_g, *, layer, which, sub, t, b_off, tm, final):
    n = x.shape[0]
    tf = 512
    kern = functools.partial(_ffn_kernel, sub=sub, final=final)
    return pl.pallas_call(
        kern,
        grid=(n // tm, D_FF // tf),
        in_specs=[
            pl.BlockSpec((tm, D_MODEL), lambda i, j: (i, 0)),
            _mod_spec(layer, tm, t, b_off),
            pl.BlockSpec((None, None, 1, D_MODEL), lambda i, j: (layer, sub, 0, 0)),
            pl.BlockSpec((None, None, D_MODEL, tf), lambda i, j: (layer, which, 0, j)),
            pl.BlockSpec((None, None, D_MODEL, tf), lambda i, j: (layer, which, 0, j)),
            pl.BlockSpec((None, None, tf, D_MODEL), lambda i, j: (layer, which, j, 0)),
            pl.BlockSpec((1, D_MODEL), lambda i, j: (0, 0)),
        ],
        out_specs=pl.BlockSpec((tm, D_MODEL), lambda i, j: (i, 0)),
        out_shape=jax.ShapeDtypeStruct((n, D_MODEL), F32),
        scratch_shapes=[pltpu.VMEM((tm, D_MODEL), BF16), pltpu.VMEM((tm, D_MODEL), F32)],
        compiler_params=pltpu.CompilerParams(
            dimension_semantics=("arbitrary", "arbitrary"), vmem_limit_bytes=VMEM_LIMIT),
        name="ffn",
    )(x, mod, norm_g, w1, w3, w2, final_g)


def _proj_kernel(x_ref, m_ref, g_ref, w_ref, o_ref, h_ref):
    @pl.when(pl.program_id(1) == 0)
    def _():
        h_ref[...] = _modulated_norm(x_ref[...], g_ref[...], m_ref[...], 1).astype(BF16)

    o_ref[...] = _dot(h_ref[...], w_ref[...])


def _in_proj(x, mod, norm_g, w_in, *, layer, t, b_off, tm):
    n = x.shape[0]
    return pl.pallas_call(
        _proj_kernel,
        grid=(n // tm, PROJ_WIDTH // PROJ_TILE),
        in_specs=[
            pl.BlockSpec((tm, D_MODEL), lambda i, j: (i, 0)),
            _mod_spec(layer, tm, t, b_off),
            pl.BlockSpec((None, None, 1, D_MODEL), lambda i, j: (layer, 1, 0, 0)),
            pl.BlockSpec((None, D_MODEL, PROJ_TILE), lambda i, j: (layer, 0, j)),
        ],
        out_specs=pl.BlockSpec((tm, PROJ_TILE), lambda i, j: (i, j)),
        out_shape=jax.ShapeDtypeStruct((n, PROJ_WIDTH), F32),
        scratch_shapes=[pltpu.VMEM((tm, D_MODEL), BF16)],
        compiler_params=pltpu.CompilerParams(
            dimension_semantics=("arbitrary", "arbitrary"), vmem_limit_bytes=VMEM_LIMIT),
        name="in_proj",
    )(x, mod, norm_g, w_in)


def _split3(x):
    hi = x.astype(BF16)
    r1 = x - hi.astype(F32)
    mid = r1.astype(BF16)
    lo = (r1 - mid.astype(F32)).astype(BF16)
    return hi, mid, lo


def _wkv_kernel(p_ref, sh0_ref, s0_ref, mu_ref, vec_ref, wlr_ref, o_ref, sout_ref, state_ref, prev_ref, *, chunk):
    c = pl.program_id(1)
    C = chunk
    C2 = 2 * C

    @pl.when(c == 0)
    def _():
        state_ref[...] = s0_ref[0]
        prev_ref[...] = sh0_ref[0]

    p = p_ref[...]
    row = lax.broadcasted_iota(jnp.int32, (C, 1), 0)
    prev = jnp.where(row == 0, prev_ref[...], pltpu.roll(p, 1, 0))
    prev_ref[...] = p[C - 1:C, :]
    xs = p + (prev - p) * mu_ref[...]

    r = xs[:, 0:RWKV_WIDTH]
    k = xs[:, RWKV_WIDTH:2 * RWKV_WIDTH]
    v = xs[:, 2 * RWKV_WIDTH:3 * RWKV_WIDTH]
    z = xs[:, 3 * RWKV_WIDTH:RWKV_COLS]
    zc = lax.broadcasted_iota(jnp.int32, z.shape, 1)
    zact = jnp.where(zc < DECAY_RANK, jnp.tanh(z),
                     jnp.where(zc < DECAY_RANK + A_RANK, z, jax.nn.sigmoid(z)))
    lr = _dot(zact.astype(BF16), wlr_ref[...])
    w0, a0, k_k, k_a, lnx_w, lnx_b, r_k = (vec_ref[i:i + 1, :] for i in range(7))

    zz = -(w0 + lr[:, 0:RWKV_WIDTH])
    softplus = jnp.maximum(zz, 0.0) + jnp.log(1.0 + jnp.exp(-jnp.abs(zz)))
    lw = -jnp.exp(-softplus - 0.5)
    a = jax.nn.sigmoid(a0 + lr[:, RWKV_WIDTH:2 * RWKV_WIDTH])
    gate = lr[:, 2 * RWKV_WIDTH:3 * RWKV_WIDTH]

    lane = lax.broadcasted_iota(jnp.int32, (1, LANES), 1)
    head0 = lane < HEAD_SIZE
    ri = lax.broadcasted_iota(jnp.int32, (LANES, LANES), 0)
    ci = lax.broadcasted_iota(jnp.int32, (LANES, LANES), 1)
    head_ones = ((ri < HEAD_SIZE) == (ci < HEAD_SIZE)).astype(BF16)

    def head_sum(x):
        st = jnp.concatenate([x[:, g * LANES:(g + 1) * LANES] for g in range(N_PAIRS)], axis=0)
        s = _dot(st.astype(BF16), head_ones)
        return jnp.concatenate([s[g * C:(g + 1) * C] for g in range(N_PAIRS)], axis=1)

    tr = lax.broadcasted_iota(jnp.int32, (C, C), 0)
    tc = lax.broadcasted_iota(jnp.int32, (C, C), 1)
    cum_ones = (tc <= tr).astype(BF16)
    hi, mid, lo = _split3(lw)
    cs = _dot(cum_ones, hi) + _dot(cum_ones, mid) + _dot(cum_ones, lo)
    w_in = jnp.exp(cs)
    w_inv = jnp.exp(-cs)
    w_ex = jnp.exp(cs - lw)

    kk = k * k_k
    k2 = k * (1.0 + (a - 1.0) * k_a)
    kk = kk * lax.rsqrt(jnp.maximum(head_sum(kk * kk), 1e-24))
    b = kk * a
    bonus = head_sum(r * k2 * r_k) * v

    r_t = r * w_in
    kk_t = kk * w_ex
    k_h = k2 * w_inv
    b_h = b * w_inv
    w_c = w_in[C - 1:C, :]
    k_b = k_h * w_c
    b_b = b_h * w_c

    sr = lax.broadcasted_iota(jnp.int32, (C2, C2), 0)
    sc = lax.broadcasted_iota(jnp.int32, (C2, C2), 1)
    same_head = (sr < C) == (sc < C)
    strict = jnp.logical_and(same_head, sc < sr)
    incl = jnp.logical_and(same_head, sc <= sr)

    def stack(x):
        return jnp.concatenate([jnp.where(head0, x, 0.0), jnp.where(head0, 0.0, x)], axis=0).astype(BF16)

    pairs = range(N_PAIRS)
    sls = [slice(g * LANES, (g + 1) * LANES) for g in pairs]
    s_prev = [state_ref[g] for g in pairs]
    q = [jnp.concatenate([stack(kk_t[:, sl]), stack(r_t[:, sl])], axis=0) for sl in sls]
    kb = [jnp.concatenate([stack(k_h[:, sl]), stack(b_h[:, sl])], axis=0) for sl in sls]
    s_v = [stack(v[:, sl]) for sl in sls]
    kb_c = [jnp.concatenate([stack(k_b[:, sl]), -stack(b_b[:, sl])], axis=0) for sl in sls]

    scores = [_dot_nt(q[g], kb[g]) for g in pairs]
    proj = [_dot_nt(q[g], s_prev[g].astype(BF16)) for g in pairs]
    a_kk_k = [jnp.where(strict, s[:C2, :C2], 0.0).astype(BF16) for s in scores]
    m_b = [jnp.where(strict, s[:C2, C2:], 0.0) for s in scores]
    a_r = [jnp.concatenate([jnp.where(incl, s[C2:, :C2], 0.0), jnp.where(incl, -s[C2:, C2:], 0.0)],
                           axis=1).astype(BF16) for s in scores]
    rhs = [_dot(a_kk_k[g], s_v[g]) + proj[g][:C2] for g in pairs]

    z = rhs
    pw = [m.astype(BF16) for m in m_b]
    sign = -1.0
    n = 1
    while n < C:
        last = 2 * n >= C
        nxt = []
        for g in pairs:
            zb = z[g].astype(BF16)
            if last:
                pz = _dot(pw[g], zb)
            else:
                both = _dot(pw[g], jnp.concatenate([pw[g], zb], axis=1))
                nxt.append(both[:, :C2].astype(BF16))
                pz = both[:, C2:]
            z[g] = z[g] + sign * pz
        pw = nxt
        sign = 1.0
        n *= 2

    sa = [x.astype(BF16) for x in z]
    v_sa = [jnp.concatenate([s_v[g], sa[g]], axis=0) for g in pairs]
    ys = []
    for g in pairs:
        y_st = proj[g][C2:] + _dot(a_r[g], v_sa[g])
        ys.append(y_st[:C] + y_st[C:])
        state_ref[g] = s_prev[g] * w_c[:, sls[g]] + _dot_tn(v_sa[g], kb_c[g])

    y = jnp.concatenate(ys, axis=1)
    mean = head_sum(y) * (1.0 / HEAD_SIZE)
    d = y - mean
    var = head_sum(d * d) * (1.0 / HEAD_SIZE)
    yn = d * lax.rsqrt(var + GN_EPS) * lnx_w + lnx_b + bonus
    o_ref[...] = (yn * gate).astype(BF16)

    @pl.when(c == pl.num_programs(1) - 1)
    def _():
        sout_ref[0] = state_ref[...]


def _wkv(p_rwkv, shift0, s0_pairs, mu, vecs, w_lowrank, *, layer, nb, t, chunk):
    n = p_rwkv.shape[0]
    nc = t // chunk
    kern = functools.partial(_wkv_kernel, chunk=chunk)
    return pl.pallas_call(
        kern,
        grid=(nb, nc),
        in_specs=[
            pl.BlockSpec((chunk, RWKV_COLS), lambda b, c: (b * nc + c, 0)),
            pl.BlockSpec((1, 1, RWKV_COLS), lambda b, c: (b, 0, 0)),
            pl.BlockSpec((1, N_PAIRS, LANES, LANES), lambda b, c: (b, 0, 0, 0)),
            pl.BlockSpec((None, 1, RWKV_COLS), lambda b, c: (layer, 0, 0)),
            pl.BlockSpec((None, 8, RWKV_WIDTH), lambda b, c: (layer, 0, 0)),
            pl.BlockSpec((None, LOWRANK_PAD, 3 * RWKV_WIDTH), lambda b, c: (layer, 0, 0)),
        ],
        out_specs=[
            pl.BlockSpec((chunk, RWKV_WIDTH), lambda b, c: (b * nc + c, 0)),
            pl.BlockSpec((1, N_PAIRS, LANES, LANES), lambda b, c: (b, 0, 0, 0)),
        ],
        out_shape=[
            jax.ShapeDtypeStruct((n, RWKV_WIDTH), BF16),
            jax.ShapeDtypeStruct((nb, N_PAIRS, LANES, LANES), F32),
        ],
        scratch_shapes=[pltpu.VMEM((N_PAIRS, LANES, LANES), F32), pltpu.VMEM((1, RWKV_COLS), F32)],
        compiler_params=pltpu.CompilerParams(
            dimension_semantics=("arbitrary", "arbitrary"), vmem_limit_bytes=VMEM_LIMIT),
        name="wkv",
    )(p_rwkv, shift0, s0_pairs, mu, vecs, w_lowrank)


def _merge_kernel(x_ref, m_ref, yg_ref, pp_ref, buf_ref, gt_ref, wo_ref, wp_ref, ps_ref, wout_ref, o_ref, carry_ref,
                  *, pos0, tm):
    i = pl.program_id(1)

    @pl.when(i == 0)
    def _():
        carry_ref[...] = buf_ref[0]

    xp = pp_ref[...]
    full = jnp.concatenate([carry_ref[...], xp], axis=0)
    carry_ref[...] = full[tm:tm + POOL_CARRY]

    pos = pos0 + i * tm + lax.broadcasted_iota(jnp.int32, (tm, POOL_GROUP), 0)
    o_b = []
    for gi, win in enumerate(POOL_WINDOWS):
        cols = slice(gi * POOL_GROUP, (gi + 1) * POOL_GROUP)
        s = full[:, cols]
        width = 1
        while width < win:
            s = s[width:] + s[:-width]
            width *= 2
        s = s[POOL_CARRY - (win - 1):]
        cnt = jnp.minimum(win, pos + 1).astype(F32)
        pooled = s / cnt - xp[:, cols]
        o_b.append(_dot(pooled.astype(BF16), wp_ref[gi]))
    o_b = jnp.concatenate(o_b, axis=1) * ps_ref[...]
    o_a = _dot(yg_ref[...], wo_ref[...])
    gates = jax.nn.sigmoid(gt_ref[...])
    merged = gates[:, :D_MODEL] * o_a + gates[:, D_MODEL:] * o_b
    o_ref[...] = x_ref[...] + m_ref[...][:, 5, :] * _dot(merged.astype(BF16), wout_ref[...])


def _merge(x, mod, yg, proj, pool0, w_o, w_pool, pool_scale, w_out, *, layer, nb, t, b_off, tm, pos0):
    n = x.shape[0]
    nt = t // tm
    kern = functools.partial(_merge_kernel, pos0=pos0, tm=tm)
    row = lambda b, i: (b * nt + i, 0)
    return pl.pallas_call(
        kern,
        grid=(nb, nt),
        in_specs=[
            pl.BlockSpec((tm, D_MODEL), row),
            pl.BlockSpec((None, 1, N_MOD, D_MODEL), lambda b, i: (layer, b_off + b, 0, 0)),
            pl.BlockSpec((tm, RWKV_WIDTH), row),
            pl.BlockSpec((tm, POOL_WIDTH), lambda b, i: (b * nt + i, POOL_COL // POOL_WIDTH)),
            pl.BlockSpec((1, POOL_CARRY, POOL_WIDTH), lambda b, i: (b, 0, 0)),
            pl.BlockSpec((tm, 2 * D_MODEL), lambda b, i: (b * nt + i, GATE_COL // (2 * D_MODEL))),
            pl.BlockSpec((None, RWKV_WIDTH, D_MODEL), lambda b, i: (layer, 0, 0)),
            pl.BlockSpec((None, len(POOL_WINDOWS), POOL_GROUP, POOL_OUT_GROUP), lambda b, i: (layer, 0, 0, 0)),
            pl.BlockSpec((None, 1, D_MODEL), lambda b, i: (layer, 0, 0)),
            pl.BlockSpec((None, D_MODEL, D_MODEL), lambda b, i: (layer, 0, 0)),
        ],
        out_specs=pl.BlockSpec((tm, D_MODEL), row),
        out_shape=jax.ShapeDtypeStruct((n, D_MODEL), F32),
        scratch_shapes=[pltpu.VMEM((POOL_CARRY, POOL_WIDTH), F32)],
        compiler_params=pltpu.CompilerParams(
            dimension_semantics=("arbitrary", "arbitrary"), vmem_limit_bytes=VMEM_LIMIT),
        name="merge",
    )(x, mod, yg, proj, pool0, proj, w_o, w_pool, pool_scale, w_out)


def _to_pairs(s):
    b = s.shape[0]
    s = s.reshape(b, N_PAIRS, 2, HEAD_SIZE, HEAD_SIZE)
    z = jnp.zeros_like(s[:, :, 0])
    top = jnp.concatenate([s[:, :, 0], z], axis=-1)
    bot = jnp.concatenate([z, s[:, :, 1]], axis=-1)
    return jnp.concatenate([top, bot], axis=-2)


def _from_pairs(sp):
    b = sp.shape[0]
    h0 = sp[:, :, :HEAD_SIZE, :HEAD_SIZE]
    h1 = sp[:, :, HEAD_SIZE:, HEAD_SIZE:]
    return jnp.stack([h0, h1], axis=2).reshape(b, N_HEADS, HEAD_SIZE, HEAD_SIZE)


def _stream_set(x, t, nb, b_off, *, ffn_tm, proj_tm, merge_tm, chunk, pos0):
    return dict(x=x.reshape(nb * t, D_MODEL), t=t, nb=nb, b_off=b_off, ffn_tm=ffn_tm, proj_tm=proj_tm,
                merge_tm=merge_tm, chunk=chunk, pos0=pos0)


def kernel(x_prompt, x_sample, c_prompt, c_sample, state_wkv, state_shift, state_pool, norm_g, w_mod, b_mod,
           ffn_w1, ffn_w3, ffn_w2, w_in, shift_mu, w0, w_decay, a0, w_a, w_gate, k_k, k_a, r_k, lnx_w, lnx_b,
           w_o_rwkv, w_pool, pool_scale, w_out, final_g):
    bp, seq, _ = x_prompt.shape
    bs, dseq, _ = x_sample.shape
    past_len = 1024
    dt = x_prompt.dtype

    w1 = ffn_w1.astype(BF16)
    w3 = ffn_w3.astype(BF16)
    w2 = ffn_w2.astype(BF16)
    pad = jnp.zeros((DEPTH, D_MODEL, GATE_COL - SHIFT_WIDTH), BF16)
    w_in_b = w_in.astype(BF16)
    w_in_p = jnp.concatenate([w_in_b[:, :, :SHIFT_WIDTH], pad, w_in_b[:, :, SHIFT_WIDTH + POOL_WIDTH:],
                              w_in_b[:, :, SHIFT_WIDTH:SHIFT_WIDTH + POOL_WIDTH]], axis=2)
    w_o_b = w_o_rwkv.astype(BF16)
    w_pool_b = w_pool.astype(BF16)
    w_out_b = w_out.astype(BF16)
    zr = lambda rows: jnp.zeros((DEPTH, rows, RWKV_WIDTH), F32)
    w_lr = jnp.concatenate([
        jnp.concatenate([w_decay, zr(LOWRANK_PAD - DECAY_RANK)], axis=1),
        jnp.concatenate([zr(DECAY_RANK), w_a, zr(LOWRANK_PAD - DECAY_RANK - A_RANK)], axis=1),
        jnp.concatenate([zr(DECAY_RANK + A_RANK), w_gate, zr(LOWRANK_PAD - LOWRANK)], axis=1),
    ], axis=2).astype(BF16)
    vecs = jnp.stack([w0, a0, k_k, k_a, lnx_w, lnx_b, r_k.reshape(DEPTH, RWKV_WIDTH),
                      jnp.zeros_like(w0)], axis=1)
    mu = jnp.pad(shift_mu, ((0, 0), (0, RWKV_COLS - SHIFT_WIDTH))).reshape(DEPTH, 1, RWKV_COLS)
    norm_g4 = norm_g.reshape(DEPTH, 3, 1, D_MODEL)
    fg = final_g.reshape(1, D_MODEL)
    ps = pool_scale.reshape(DEPTH, 1, D_MODEL)

    mod = _modulation(jnp.concatenate([c_prompt, c_sample], axis=0), w_mod, b_mod)
    mod = mod.reshape(DEPTH, bp + bs, N_MOD, D_MODEL)

    prompt = _stream_set(x_prompt, seq, bp, 0, ffn_tm=min(512, seq), proj_tm=min(1024, seq),
                         merge_tm=min(256, seq), chunk=min(64, seq), pos0=0)
    sample = _stream_set(x_sample, dseq, bs, bp, ffn_tm=bs * dseq, proj_tm=bs * dseq, merge_tm=dseq,
                         chunk=dseq, pos0=past_len)

    outs = {"p": ([], [], []), "s": ([], [], [])}
    for l in range(DEPTH):
        for tag, st in (("p", prompt), ("s", sample)):
            nb, t = st["nb"], st["t"]
            if tag == "p":
                shift0 = jnp.zeros((nb, 1, RWKV_COLS), F32)
                s0 = jnp.zeros((nb, N_PAIRS, LANES, LANES), F32)
                pool0 = jnp.zeros((nb, POOL_CARRY, POOL_WIDTH), F32)
            else:
                shift0 = jnp.pad(state_shift[l], ((0, 0), (0, RWKV_COLS - SHIFT_WIDTH))).reshape(nb, 1, RWKV_COLS)
                s0 = _to_pairs(state_wkv[l].astype(F32))
                pool0 = jnp.pad(state_pool[l], ((0, 0), (POOL_CARRY - POOL_BUF, 0), (0, 0)))
            common = dict(layer=l, t=t, b_off=st["b_off"])
            x = st["x"]
            x = _ffn(x, mod, norm_g4, w1, w3, w2, fg, which=0, sub=0, tm=st["ffn_tm"], final=False, **common)
            proj = _in_proj(x, mod, norm_g4, w_in_p, tm=st["proj_tm"], **common)
            yg, s_new = _wkv(proj, shift0, s0, mu, vecs, w_lr, layer=l, nb=nb, t=t, chunk=st["chunk"])
            x = _merge(x, mod, yg, proj, pool0, w_o_b, w_pool_b, ps, w_out_b, nb=nb,
                       tm=st["merge_tm"], pos0=st["pos0"], **common)
            x = _ffn(x, mod, norm_g4, w1, w3, w2, fg, which=1, sub=2, tm=st["ffn_tm"],
                     final=(l == DEPTH - 1), **common)
            st["x"] = x
            wkv_l, shift_l, pool_l = outs[tag]
            wkv_l.append(_from_pairs(s_new).astype(dt))
            proj3 = proj.reshape(nb, t, PROJ_WIDTH)
            shift_l.append(proj3[:, -1, :SHIFT_WIDTH])
            pool_l.append(proj3[:, t - POOL_BUF:, POOL_COL:])

    yp = prompt["x"].reshape(bp, seq, D_MODEL)
    ys = sample["x"].reshape(bs, dseq, D_MODEL)
    pw, psh, ppl = (jnp.stack(v) for v in outs["p"])
    sw, ssh, spl = (jnp.stack(v) for v in outs["s"])
    return (yp, ys, pw, psh, ppl, sw, ssh, spl)
```

```python
import functools

import jax
import jax.numpy as jnp
from jax import lax
from jax.experimental import pallas as pl
from jax.experimental.pallas import tpu as pltpu

F32 = jnp.float32
BF16 = jnp.bfloat16

D_MODEL = 2048
DEPTH = 4
RWKV_WIDTH = D_MODEL // 2
HEAD_SIZE = 64
N_HEADS = RWKV_WIDTH // HEAD_SIZE
DECAY_RANK = 64
A_RANK = 64
GATE_RANK = 160
POOL_WIDTH = D_MODEL // 2
POOL_WINDOWS = (2, 4, 8, 16)
POOL_GROUP = POOL_WIDTH // len(POOL_WINDOWS)
POOL_OUT_GROUP = D_MODEL // len(POOL_WINDOWS)
POOL_BUF = max(POOL_WINDOWS) - 1
D_FF = ((8 * D_MODEL // 3 + 255) // 256) * 256
SHIFT_WIDTH = 3 * RWKV_WIDTH + DECAY_RANK + A_RANK + GATE_RANK
N_MOD = 9
RMS_EPS = 1e-6
GN_EPS = 64e-5

LANES = 128
N_PAIRS = RWKV_WIDTH // LANES
LOWRANK = DECAY_RANK + A_RANK + GATE_RANK
LOWRANK_PAD = 3 * LANES
RWKV_COLS = 3 * RWKV_WIDTH + LOWRANK_PAD
PROJ_TILE = 1024
GATE_COL = 2 * D_MODEL
POOL_COL = 4 * D_MODEL
PROJ_WIDTH = POOL_COL + POOL_WIDTH
POOL_CARRY = 16
VMEM_LIMIT = 56 * 1024 * 1024
FFN_VMEM_LIMIT = 62 * 1024 * 1024


def _dot(a, b):
    return jnp.dot(a, b, preferred_element_type=F32)


def _dot_nt(a, b):
    return lax.dot_general(a, b, (((1,), (1,)), ((), ())), preferred_element_type=F32)


def _dot_tn(a, b):
    return lax.dot_general(a, b, (((0,), (0,)), ((), ())), preferred_element_type=F32)


def _rows(v, tm):
    g, d = v.shape
    if g == 1:
        return v
    return jnp.broadcast_to(v[:, None, :], (g, tm // g, d)).reshape(tm, d)


NORM_ROWS = 16


def _store_modulated_norm(h_ref, x_ref, g_ref, m_ref, sub):
    tm = x_ref.shape[0]
    n_streams = m_ref.shape[0]
    rows_per_stream = tm // n_streams
    gain = g_ref[...]

    def body(r, carry):
        r0 = pl.multiple_of(r * NORM_ROWS, NORM_ROWS)
        s = 0 if n_streams == 1 else r0 // rows_per_stream
        x = x_ref[pl.ds(r0, NORM_ROWS), :]
        scale = m_ref[s, pl.ds(3 * sub + 1, 1), :]
        shift = m_ref[s, pl.ds(3 * sub, 1), :]
        y = x * lax.rsqrt(jnp.mean(x * x, axis=-1, keepdims=True) + RMS_EPS) * gain
        h_ref[pl.ds(r0, NORM_ROWS), :] = (y * (1.0 + scale) + shift).astype(BF16)
        return carry

    lax.fori_loop(0, tm // NORM_ROWS, body, 0, unroll=8)


def _mod_kernel(c_ref, w_ref, b_ref, o_ref):
    c = c_ref[...]
    s = (c * jax.nn.sigmoid(c)).astype(BF16)
    o_ref[0] = _dot(s, w_ref[0].astype(BF16)) + b_ref[0]


def _modulation(c_all, w_mod, b_mod):
    nb = c_all.shape[0]
    width = N_MOD * D_MODEL
    tn = 1024
    return pl.pallas_call(
        _mod_kernel,
        grid=(DEPTH, width // tn),
        in_specs=[
            pl.BlockSpec((nb, D_MODEL), lambda l, j: (0, 0)),
            pl.BlockSpec((1, D_MODEL, tn), lambda l, j: (l, 0, j)),
            pl.BlockSpec((1, 1, tn), lambda l, j: (l, 0, j)),
        ],
        out_specs=pl.BlockSpec((1, nb, tn), lambda l, j: (l, 0, j)),
        out_shape=jax.ShapeDtypeStruct((DEPTH, nb, width), F32),
        compiler_params=pltpu.CompilerParams(
            dimension_semantics=("arbitrary", "arbitrary"), vmem_limit_bytes=VMEM_LIMIT),
        name="modulation",
    )(c_all, w_mod, b_mod.reshape(DEPTH, 1, width))


def _mod_spec(layer, tm, t, b_off):
    g = max(1, tm // t)
    tiles_per_stream = max(1, t // tm)
    assert b_off % g == 0 and (tm % t == 0 or t % tm == 0)
    base = b_off // g
    return pl.BlockSpec((None, g, N_MOD, D_MODEL),
                        lambda i, j: (layer, base + i // tiles_per_stream, 0, 0))


FF_TILE = 512


def _ffn_kernel(x_ref, m_ref, g_ref, w1_ref, w3_ref, w2_ref, fg_ref, o_ref, h_ref, *, sub, final):
    j = pl.program_id(1)

    def ff_tile():
        h = h_ref[...]
        a = _dot(h, w1_ref[...])
        b = _dot(h, w3_ref[...])
        u = (a * jax.nn.sigmoid(a) * b).astype(BF16)
        return _dot(u, w2_ref[...])

    @pl.when(j == 0)
    def _():
        _store_modulated_norm(h_ref, x_ref, g_ref, m_ref, sub)
        o_ref[...] = ff_tile()

    @pl.when(j > 0)
    def _():
        o_ref[...] += ff_tile()

    @pl.when(j == pl.num_programs(1) - 1)
    def _():
        tm = x_ref.shape[0]
        gate = _rows(m_ref[...][:, 3 * sub + 2, :], tm)
        y = x_ref[...] + (0.5 * gate) * o_ref[...]
        if final:
            y = y * lax.rsqrt(jnp.mean(y * y, axis=-1, keepdims=True) + RMS_EPS) * fg_ref[...]
        o_ref[...] = y


def _ffn(x, mod, norm_g, w1, w3, w2, final_g, *, layer, which, sub, t, b_off, tm, final):
    n = x.shape[0]
    tf = FF_TILE
    kern = functools.partial(_ffn_kernel, sub=sub, final=final)
    return pl.pallas_call(
        kern,
        grid=(n // tm, D_FF // tf),
        in_specs=[
            pl.BlockSpec((tm, D_MODEL), lambda i, j: (i, 0)),
            _mod_spec(layer, tm, t, b_off),
            pl.BlockSpec((None, None, 1, D_MODEL), lambda i, j: (layer, sub, 0, 0)),
            pl.BlockSpec((None, None, D_MODEL, tf), lambda i, j: (layer, which, 0, j)),
            pl.BlockSpec((None, None, D_MODEL, tf), lambda i, j: (layer, which, 0, j)),
            pl.BlockSpec((None, None, tf, D_MODEL), lambda i, j: (layer, which, j, 0)),
            pl.BlockSpec((1, D_MODEL), lambda i, j: (0, 0)),
        ],
        out_specs=pl.BlockSpec((tm, D_MODEL), lambda i, j: (i, 0)),
        out_shape=jax.ShapeDtypeStruct((n, D_MODEL), F32),
        scratch_shapes=[pltpu.VMEM((tm, D_MODEL), BF16)],
        compiler_params=pltpu.CompilerParams(
            dimension_semantics=("arbitrary", "arbitrary"), vmem_limit_bytes=FFN_VMEM_LIMIT),
        name="ffn",
    )(x, mod, norm_g, w1, w3, w2, final_g)


def _proj_kernel(x_ref, m_ref, g_ref, w_ref, o_ref, h_ref):
    @pl.when(pl.program_id(1) == 0)
    def _():
        _store_modulated_norm(h_ref, x_ref, g_ref, m_ref, 1)

    o_ref[...] = _dot(h_ref[...], w_ref[...])


def _in_proj(x, mod, norm_g, w_in, *, layer, t, b_off, tm):
    n = x.shape[0]
    return pl.pallas_call(
        _proj_kernel,
        grid=(n // tm, PROJ_WIDTH // PROJ_TILE),
        in_specs=[
            pl.BlockSpec((tm, D_MODEL), lambda i, j: (i, 0)),
            _mod_spec(layer, tm, t, b_off),
            pl.BlockSpec((None, None, 1, D_MODEL), lambda i, j: (layer, 1, 0, 0)),
            pl.BlockSpec((None, D_MODEL, PROJ_TILE), lambda i, j: (layer, 0, j)),
        ],
        out_specs=pl.BlockSpec((tm, PROJ_TILE), lambda i, j: (i, j)),
        out_shape=jax.ShapeDtypeStruct((n, PROJ_WIDTH), F32),
        scratch_shapes=[pltpu.VMEM((tm, D_MODEL), BF16)],
        compiler_params=pltpu.CompilerParams(
            dimension_semantics=("arbitrary", "arbitrary"), vmem_limit_bytes=VMEM_LIMIT),
        name="in_proj",
    )(x, mod, norm_g, w_in)


def _split3(x):
    hi = x.astype(BF16)
    r1 = x - hi.astype(F32)
    mid = r1.astype(BF16)
    lo = (r1 - mid.astype(F32)).astype(BF16)
    return hi, mid, lo


def _wkv_kernel(p_ref, sh0_ref, s0_ref, mu_ref, vec_ref, wlr_ref, o_ref, sout_ref, state_ref, prev_ref, *, chunk):
    c = pl.program_id(1)
    C = chunk
    C2 = 2 * C
    n_streams = p_ref.shape[1]

    @pl.when(c == 0)
    def _():
        state_ref[...] = s0_ref[0]
        prev_ref[...] = sh0_ref[0]

    w0, a0, k_k, k_a, lnx_w, lnx_b, r_k = (vec_ref[i:i + 1, :] for i in range(7))
    lane = lax.broadcasted_iota(jnp.int32, (1, LANES), 1)
    head0 = lane < HEAD_SIZE
    ri = lax.broadcasted_iota(jnp.int32, (LANES, LANES), 0)
    ci = lax.broadcasted_iota(jnp.int32, (LANES, LANES), 1)
    head_ones = ((ri < HEAD_SIZE) == (ci < HEAD_SIZE)).astype(BF16)
    tr = lax.broadcasted_iota(jnp.int32, (C, C), 0)
    tc = lax.broadcasted_iota(jnp.int32, (C, C), 1)
    cum_ones = (tc <= tr).astype(BF16)
    row = lax.broadcasted_iota(jnp.int32, (C, 1), 0)

    def head_sum(x):
        st = jnp.concatenate([x[:, g * LANES:(g + 1) * LANES] for g in range(N_PAIRS)], axis=0)
        s = _dot(st.astype(BF16), head_ones)
        return jnp.concatenate([s[g * C:(g + 1) * C] for g in range(N_PAIRS)], axis=1)

    def prepare(s):
        p = p_ref[0, s]
        prev = jnp.where(row == 0, prev_ref[s], pltpu.roll(p, 1, 0))
        prev_ref[s] = p[C - 1:C, :]
        xs = p + (prev - p) * mu_ref[...]

        r = xs[:, 0:RWKV_WIDTH]
        k = xs[:, RWKV_WIDTH:2 * RWKV_WIDTH]
        v = xs[:, 2 * RWKV_WIDTH:3 * RWKV_WIDTH]
        z = xs[:, 3 * RWKV_WIDTH:RWKV_COLS]
        zc = lax.broadcasted_iota(jnp.int32, z.shape, 1)
        zact = jnp.where(zc < DECAY_RANK, jnp.tanh(z),
                         jnp.where(zc < DECAY_RANK + A_RANK, z, jax.nn.sigmoid(z)))
        lr = _dot(zact.astype(BF16), wlr_ref[...])

        zz = -(w0 + lr[:, 0:RWKV_WIDTH])
        softplus = jnp.maximum(zz, 0.0) + jnp.log(1.0 + jnp.exp(-jnp.abs(zz)))
        lw = -jnp.exp(-softplus - 0.5)
        a = jax.nn.sigmoid(a0 + lr[:, RWKV_WIDTH:2 * RWKV_WIDTH])

        hi, mid, lo = _split3(lw)
        cs = _dot(cum_ones, hi) + _dot(cum_ones, mid) + _dot(cum_ones, lo)
        w_in = jnp.exp(cs)
        w_inv = jnp.exp(-cs)
        w_ex = jnp.exp(cs - lw)

        kk = k * k_k
        k2 = k * (1.0 + (a - 1.0) * k_a)
        kk = kk * lax.rsqrt(jnp.maximum(head_sum(kk * kk), 1e-24))
        b = kk * a
        k_h = k2 * w_inv
        b_h = b * w_inv
        w_c = w_in[C - 1:C, :]
        return dict(v=v, r_t=r * w_in, kk_t=kk * w_ex, k_h=k_h, b_h=b_h, w_c=w_c, k_b=k_h * w_c, b_b=b_h * w_c,
                    gate=lr[:, 2 * RWKV_WIDTH:3 * RWKV_WIDTH], bonus=head_sum(r * k2 * r_k) * v)

    pre = [prepare(s) for s in range(n_streams)]

    tr2 = lax.broadcasted_iota(jnp.int32, (C, C2), 0)
    tc2 = lax.broadcasted_iota(jnp.int32, (C, C2), 1)
    col_head0 = tc2 < C
    ts2 = jnp.where(col_head0, tc2, tc2 - C)
    strict = ts2 < tr2
    incl = ts2 <= tr2

    def stack(x):
        return jnp.concatenate([jnp.where(head0, x, 0.0), jnp.where(head0, 0.0, x)], axis=0).astype(BF16)

    def block_diag(p):
        return jnp.concatenate([jnp.where(col_head0, p, 0.0), jnp.where(col_head0, 0.0, p)], axis=0).astype(BF16)

    chains = [(s, g) for s in range(n_streams) for g in range(N_PAIRS)]
    pairs = range(len(chains))

    def lanes(name, i):
        s, g = chains[i]
        return pre[s][name][:, g * LANES:(g + 1) * LANES]

    s_prev = [state_ref[s, g] for s, g in chains]
    q = [jnp.concatenate([lanes("kk_t", i), lanes("r_t", i)], axis=0).astype(BF16) for i in pairs]
    kb = [jnp.concatenate([stack(lanes("k_h", i)), stack(lanes("b_h", i))], axis=0) for i in pairs]
    s_v = [stack(lanes("v", i)) for i in pairs]
    kb_c = [jnp.concatenate([stack(lanes("k_b", i)), -stack(lanes("b_b", i))], axis=0) for i in pairs]

    scores = [_dot_nt(q[g], kb[g]) for g in pairs]
    proj = [_dot_nt(q[g], s_prev[g].astype(BF16)) for g in pairs]
    a_kk_k = [jnp.where(strict, s[:C, :C2], 0.0).astype(BF16) for s in scores]
    m_b = [jnp.where(strict, s[:C, C2:], 0.0) for s in scores]
    a_r = [jnp.concatenate([jnp.where(incl, s[C:, :C2], 0.0), jnp.where(incl, -s[C:, C2:], 0.0)],
                           axis=1).astype(BF16) for s in scores]
    z = [_dot(a_kk_k[g], s_v[g]) + proj[g][:C] for g in pairs]

    pw = m_b
    sign = -1.0
    n = 1
    while n < C:
        last = 2 * n >= C
        nxt = []
        for g in pairs:
            if last:
                pz = _dot(pw[g].astype(BF16), stack(z[g]))
            else:
                both = _dot(pw[g].astype(BF16), jnp.concatenate([block_diag(pw[g]), stack(z[g])], axis=1))
                nxt.append(both[:, :C2])
                pz = both[:, C2:]
            z[g] = z[g] + sign * pz
        pw = nxt
        sign = 1.0
        n *= 2

    v_sa = [jnp.concatenate([s_v[g], stack(z[g])], axis=0) for g in pairs]
    ys = []
    for i in pairs:
        s, g = chains[i]
        ys.append(proj[i][C:] + _dot(a_r[i], v_sa[i]))
        state_ref[s, g] = s_prev[i] * lanes("w_c", i) + _dot_tn(v_sa[i], kb_c[i])

    for s in range(n_streams):
        y = jnp.concatenate(ys[s * N_PAIRS:(s + 1) * N_PAIRS], axis=1)
        mean = head_sum(y) * (1.0 / HEAD_SIZE)
        d = y - mean
        var = head_sum(d * d) * (1.0 / HEAD_SIZE)
        yn = d * lax.rsqrt(var + GN_EPS) * lnx_w + lnx_b + pre[s]["bonus"]
        o_ref[0, s] = (yn * pre[s]["gate"]).astype(BF16)

    @pl.when(c == pl.num_programs(1) - 1)
    def _():
        sout_ref[0] = state_ref[...]


WKV_STREAMS = 2


def _wkv(proj, shift0, s0_pairs, mu, vecs, w_lowrank, *, layer, nb, t, chunk):
    ns = WKV_STREAMS
    nc = t // chunk
    kern = functools.partial(_wkv_kernel, chunk=chunk)
    grouped = lambda a: a.reshape((nb // ns, ns) + a.shape[1:])
    yg, s_new = pl.pallas_call(
        kern,
        grid=(nb // ns, nc),
        in_specs=[
            pl.BlockSpec((1, ns, chunk, RWKV_COLS), lambda b, c: (b, 0, c, 0)),
            pl.BlockSpec((1, ns, 1, RWKV_COLS), lambda b, c: (b, 0, 0, 0)),
            pl.BlockSpec((1, ns, N_PAIRS, LANES, LANES), lambda b, c: (b, 0, 0, 0, 0)),
            pl.BlockSpec((None, 1, RWKV_COLS), lambda b, c: (layer, 0, 0)),
            pl.BlockSpec((None, 8, RWKV_WIDTH), lambda b, c: (layer, 0, 0)),
            pl.BlockSpec((None, LOWRANK_PAD, 3 * RWKV_WIDTH), lambda b, c: (layer, 0, 0)),
        ],
        out_specs=[
            pl.BlockSpec((1, ns, chunk, RWKV_WIDTH), lambda b, c: (b, 0, c, 0)),
            pl.BlockSpec((1, ns, N_PAIRS, LANES, LANES), lambda b, c: (b, 0, 0, 0, 0)),
        ],
        out_shape=[
            jax.ShapeDtypeStruct((nb // ns, ns, t, RWKV_WIDTH), BF16),
            jax.ShapeDtypeStruct((nb // ns, ns, N_PAIRS, LANES, LANES), F32),
        ],
        scratch_shapes=[pltpu.VMEM((ns, N_PAIRS, LANES, LANES), F32), pltpu.VMEM((ns, 1, RWKV_COLS), F32)],
        compiler_params=pltpu.CompilerParams(
            dimension_semantics=("arbitrary", "arbitrary"), vmem_limit_bytes=VMEM_LIMIT),
        name="wkv",
    )(proj.reshape(nb // ns, ns, t, PROJ_WIDTH), grouped(shift0), grouped(s0_pairs), mu, vecs, w_lowrank)
    return yg.reshape(nb * t, RWKV_WIDTH), s_new.reshape(nb, N_PAIRS, LANES, LANES)


def _merge_kernel(x_ref, m_ref, yg_ref, pp_ref, buf_ref, gt_ref, wo_ref, wp_ref, ps_ref, wout_ref, o_ref, carry_ref,
                  *, pos0, tm):
    i = pl.program_id(1)

    @pl.when(i == 0)
    def _():
        carry_ref[...] = buf_ref[0]

    xp = pp_ref[...]
    full = jnp.concatenate([carry_ref[...], xp], axis=0)
    carry_ref[...] = full[tm:tm + POOL_CARRY]

    pos = pos0 + i * tm + lax.broadcasted_iota(jnp.int32, (tm, POOL_GROUP), 0)
    o_b = []
    for gi, win in enumerate(POOL_WINDOWS):
        cols = slice(gi * POOL_GROUP, (gi + 1) * POOL_GROUP)
        s = full[:, cols]
        width = 1
        while width < win:
            s = s[width:] + s[:-width]
            width *= 2
        s = s[POOL_CARRY - (win - 1):]
        cnt = jnp.minimum(win, pos + 1).astype(F32)
        pooled = s / cnt - xp[:, cols]
        o_b.append(_dot(pooled.astype(BF16), wp_ref[gi]))
    o_b = jnp.concatenate(o_b, axis=1) * ps_ref[...]
    o_a = _dot(yg_ref[...], wo_ref[...])
    gates = jax.nn.sigmoid(gt_ref[...])
    merged = gates[:, :D_MODEL] * o_a + gates[:, D_MODEL:] * o_b
    o_ref[...] = x_ref[...] + m_ref[...][:, 5, :] * _dot(merged.astype(BF16), wout_ref[...])


def _merge(x, mod, yg, proj, pool0, w_o, w_pool, pool_scale, w_out, *, layer, nb, t, b_off, tm, pos0):
    n = x.shape[0]
    nt = t // tm
    kern = functools.partial(_merge_kernel, pos0=pos0, tm=tm)
    row = lambda b, i: (b * nt + i, 0)
    return pl.pallas_call(
        kern,
        grid=(nb, nt),
        in_specs=[
            pl.BlockSpec((tm, D_MODEL), row),
            pl.BlockSpec((None, 1, N_MOD, D_MODEL), lambda b, i: (layer, b_off + b, 0, 0)),
            pl.BlockSpec((tm, RWKV_WIDTH), row),
            pl.BlockSpec((tm, POOL_WIDTH), lambda b, i: (b * nt + i, POOL_COL // POOL_WIDTH)),
            pl.BlockSpec((1, POOL_CARRY, POOL_WIDTH), lambda b, i: (b, 0, 0)),
            pl.BlockSpec((tm, 2 * D_MODEL), lambda b, i: (b * nt + i, GATE_COL // (2 * D_MODEL))),
            pl.BlockSpec((None, RWKV_WIDTH, D_MODEL), lambda b, i: (layer, 0, 0)),
            pl.BlockSpec((None, len(POOL_WINDOWS), POOL_GROUP, POOL_OUT_GROUP), lambda b, i: (layer, 0, 0, 0)),
            pl.BlockSpec((None, 1, D_MODEL), lambda b, i: (layer, 0, 0)),
            pl.BlockSpec((None, D_MODEL, D_MODEL), lambda b, i: (layer, 0, 0)),
        ],
        out_specs=pl.BlockSpec((tm, D_MODEL), row),
        out_shape=jax.ShapeDtypeStruct((n, D_MODEL), F32),
        scratch_shapes=[pltpu.VMEM((POOL_CARRY, POOL_WIDTH), F32)],
        compiler_params=pltpu.CompilerParams(
            dimension_semantics=("arbitrary", "arbitrary"), vmem_limit_bytes=VMEM_LIMIT),
        name="merge",
    )(x, mod, yg, proj, pool0, proj, w_o, w_pool, pool_scale, w_out)


def _to_pairs(s):
    b = s.shape[0]
    s = s.reshape(b, N_PAIRS, 2, HEAD_SIZE, HEAD_SIZE)
    z = jnp.zeros_like(s[:, :, 0])
    top = jnp.concatenate([s[:, :, 0], z], axis=-1)
    bot = jnp.concatenate([z, s[:, :, 1]], axis=-1)
    return jnp.concatenate([top, bot], axis=-2)


def _from_pairs(sp):
    b = sp.shape[0]
    h0 = sp[:, :, :HEAD_SIZE, :HEAD_SIZE]
    h1 = sp[:, :, HEAD_SIZE:, HEAD_SIZE:]
    return jnp.stack([h0, h1], axis=2).reshape(b, N_HEADS, HEAD_SIZE, HEAD_SIZE)


def _stream_set(x, t, nb, b_off, *, ffn_tm, proj_tm, merge_tm, chunk, pos0):
    return dict(x=x.reshape(nb * t, D_MODEL), t=t, nb=nb, b_off=b_off, ffn_tm=ffn_tm, proj_tm=proj_tm,
                merge_tm=merge_tm, chunk=chunk, pos0=pos0)


def kernel(x_prompt, x_sample, c_prompt, c_sample, state_wkv, state_shift, state_pool, norm_g, w_mod, b_mod,
           ffn_w1, ffn_w3, ffn_w2, w_in, shift_mu, w0, w_decay, a0, w_a, w_gate, k_k, k_a, r_k, lnx_w, lnx_b,
           w_o_rwkv, w_pool, pool_scale, w_out, final_g):
    bp, seq, _ = x_prompt.shape
    bs, dseq, _ = x_sample.shape
    past_len = 1024
    dt = x_prompt.dtype

    w1 = ffn_w1.astype(BF16)
    w3 = ffn_w3.astype(BF16)
    w2 = ffn_w2.astype(BF16)
    pad = jnp.zeros((DEPTH, D_MODEL, GATE_COL - SHIFT_WIDTH), BF16)
    w_in_b = w_in.astype(BF16)
    w_in_p = jnp.concatenate([w_in_b[:, :, :SHIFT_WIDTH], pad, w_in_b[:, :, SHIFT_WIDTH + POOL_WIDTH:],
                              w_in_b[:, :, SHIFT_WIDTH:SHIFT_WIDTH + POOL_WIDTH]], axis=2)
    w_o_b = w_o_rwkv.astype(BF16)
    w_pool_b = w_pool.astype(BF16)
    w_out_b = w_out.astype(BF16)
    zr = lambda rows: jnp.zeros((DEPTH, rows, RWKV_WIDTH), F32)
    w_lr = jnp.concatenate([
        jnp.concatenate([w_decay, zr(LOWRANK_PAD - DECAY_RANK)], axis=1),
        jnp.concatenate([zr(DECAY_RANK), w_a, zr(LOWRANK_PAD - DECAY_RANK - A_RANK)], axis=1),
        jnp.concatenate([zr(DECAY_RANK + A_RANK), w_gate, zr(LOWRANK_PAD - LOWRANK)], axis=1),
    ], axis=2).astype(BF16)
    vecs = jnp.stack([w0, a0, k_k, k_a, lnx_w, lnx_b, r_k.reshape(DEPTH, RWKV_WIDTH),
                      jnp.zeros_like(w0)], axis=1)
    mu = jnp.pad(shift_mu, ((0, 0), (0, RWKV_COLS - SHIFT_WIDTH))).reshape(DEPTH, 1, RWKV_COLS)
    norm_g4 = norm_g.reshape(DEPTH, 3, 1, D_MODEL)
    fg = final_g.reshape(1, D_MODEL)
    ps = pool_scale.reshape(DEPTH, 1, D_MODEL)

    mod = _modulation(jnp.concatenate([c_prompt, c_sample], axis=0), w_mod, b_mod)
    mod = mod.reshape(DEPTH, bp + bs, N_MOD, D_MODEL)

    prompt = _stream_set(x_prompt, seq, bp, 0, ffn_tm=min(1024, seq), proj_tm=min(1024, seq),
                         merge_tm=min(256, seq), chunk=min(64, seq), pos0=0)
    sample = _stream_set(x_sample, dseq, bs, bp, ffn_tm=bs * dseq, proj_tm=bs * dseq, merge_tm=dseq,
                         chunk=dseq, pos0=past_len)

    outs = {"p": ([], [], []), "s": ([], [], [])}
    for l in range(DEPTH):
        for tag, st in (("p", prompt), ("s", sample)):
            nb, t = st["nb"], st["t"]
            if tag == "p":
                shift0 = jnp.zeros((nb, 1, RWKV_COLS), F32)
                s0 = jnp.zeros((nb, N_PAIRS, LANES, LANES), F32)
                pool0 = jnp.zeros((nb, POOL_CARRY, POOL_WIDTH), F32)
            else:
                shift0 = jnp.pad(state_shift[l], ((0, 0), (0, RWKV_COLS - SHIFT_WIDTH))).reshape(nb, 1, RWKV_COLS)
                s0 = _to_pairs(state_wkv[l].astype(F32))
                pool0 = jnp.pad(state_pool[l], ((0, 0), (POOL_CARRY - POOL_BUF, 0), (0, 0)))
            common = dict(layer=l, t=t, b_off=st["b_off"])
            x = st["x"]
            x = _ffn(x, mod, norm_g4, w1, w3, w2, fg, which=0, sub=0, tm=st["ffn_tm"], final=False, **common)
            proj = _in_proj(x, mod, norm_g4, w_in_p, tm=st["proj_tm"], **common)
            yg, s_new = _wkv(proj, shift0, s0, mu, vecs, w_lr, layer=l, nb=nb, t=t, chunk=st["chunk"])
            x = _merge(x, mod, yg, proj, pool0, w_o_b, w_pool_b, ps, w_out_b, nb=nb,
                       tm=st["merge_tm"], pos0=st["pos0"], **common)
            x = _ffn(x, mod, norm_g4, w1, w3, w2, fg, which=1, sub=2, tm=st["ffn_tm"],
                     final=(l == DEPTH - 1), **common)
            st["x"] = x
            wkv_l, shift_l, pool_l = outs[tag]
            wkv_l.append(_from_pairs(s_new).astype(dt))
            proj3 = proj.reshape(nb, t, PROJ_WIDTH)
            shift_l.append(proj3[:, -1, :SHIFT_WIDTH])
            pool_l.append(proj3[:, t - POOL_BUF:, POOL_COL:])

    yp = prompt["x"].reshape(bp, seq, D_MODEL)
    ys = sample["x"].reshape(bs, dseq, D_MODEL)
    pw, psh, ppl = (jnp.stack(v) for v in outs["p"])
    sw, ssh, spl = (jnp.stack(v) for v in outs["s"])
    return (yp, ys, pw, psh, ppl, sw, ssh, spl)
```

```python
import functools

import jax
import jax.numpy as jnp
from jax import lax
from jax.experimental import pallas as pl
from jax.experimental.pallas import tpu as pltpu

F32 = jnp.float32
BF16 = jnp.bfloat16

D_MODEL = 2048
DEPTH = 4
RWKV_WIDTH = D_MODEL // 2
HEAD_SIZE = 64
N_HEADS = RWKV_WIDTH // HEAD_SIZE
DECAY_RANK = 64
A_RANK = 64
GATE_RANK = 160
POOL_WIDTH = D_MODEL // 2
POOL_WINDOWS = (2, 4, 8, 16)
POOL_GROUP = POOL_WIDTH // len(POOL_WINDOWS)
POOL_OUT_GROUP = D_MODEL // len(POOL_WINDOWS)
POOL_BUF = max(POOL_WINDOWS) - 1
D_FF = ((8 * D_MODEL // 3 + 255) // 256) * 256
SHIFT_WIDTH = 3 * RWKV_WIDTH + DECAY_RANK + A_RANK + GATE_RANK
N_MOD = 9
RMS_EPS = 1e-6
GN_EPS = 64e-5

LANES = 128
N_PAIRS = RWKV_WIDTH // LANES
LOWRANK = DECAY_RANK + A_RANK + GATE_RANK
LOWRANK_PAD = 3 * LANES
RWKV_COLS = 3 * RWKV_WIDTH + LOWRANK_PAD
PROJ_TILE = 1024
GATE_COL = 2 * D_MODEL
POOL_COL = 4 * D_MODEL
PROJ_WIDTH = POOL_COL + POOL_WIDTH
POOL_CARRY = 16
VMEM_LIMIT = 56 * 1024 * 1024
FFN_VMEM_LIMIT = 62 * 1024 * 1024


def _dot(a, b):
    return jnp.dot(a, b, preferred_element_type=F32)


def _dot_nt(a, b):
    return lax.dot_general(a, b, (((1,), (1,)), ((), ())), preferred_element_type=F32)


def _dot_tn(a, b):
    return lax.dot_general(a, b, (((0,), (0,)), ((), ())), preferred_element_type=F32)


def _rows(v, tm):
    g, d = v.shape
    if g == 1:
        return v
    return jnp.broadcast_to(v[:, None, :], (g, tm // g, d)).reshape(tm, d)


NORM_ROWS = 16


ROW_SPLIT = 4


def _modulated_norm_blocks(h_ref, x_ref, g_ref, m_ref, sub):
    tm = x_ref.shape[0]
    rows_per_stream = tm // m_ref.shape[0]
    block = tm // ROW_SPLIT
    gain = g_ref[...]
    cache = {}

    def scale_shift(s):
        if s not in cache:
            cache[s] = (gain * (1.0 + m_ref[s, 3 * sub + 1:3 * sub + 2, :]), m_ref[s, 3 * sub:3 * sub + 1, :])
        return cache[s]

    def store(r):
        for r0 in range(r * block, (r + 1) * block, NORM_ROWS):
            gs, shift = scale_shift(r0 // rows_per_stream)
            x = x_ref[r0:r0 + NORM_ROWS, :]
            y = x * lax.rsqrt(jnp.mean(x * x, axis=-1, keepdims=True) + RMS_EPS)
            h_ref[r0:r0 + NORM_ROWS, :] = (y * gs + shift).astype(BF16)

    return store, block


def _mod_kernel(c_ref, w_ref, b_ref, o_ref):
    c = c_ref[...]
    s = (c * jax.nn.sigmoid(c)).astype(BF16)
    o_ref[0] = _dot(s, w_ref[0].astype(BF16)) + b_ref[0]


def _modulation(c_all, w_mod, b_mod):
    nb = c_all.shape[0]
    width = N_MOD * D_MODEL
    tn = 1024
    return pl.pallas_call(
        _mod_kernel,
        grid=(DEPTH, width // tn),
        in_specs=[
            pl.BlockSpec((nb, D_MODEL), lambda l, j: (0, 0)),
            pl.BlockSpec((1, D_MODEL, tn), lambda l, j: (l, 0, j)),
            pl.BlockSpec((1, 1, tn), lambda l, j: (l, 0, j)),
        ],
        out_specs=pl.BlockSpec((1, nb, tn), lambda l, j: (l, 0, j)),
        out_shape=jax.ShapeDtypeStruct((DEPTH, nb, width), F32),
        compiler_params=pltpu.CompilerParams(
            dimension_semantics=("arbitrary", "arbitrary"), vmem_limit_bytes=VMEM_LIMIT),
        name="modulation",
    )(c_all, w_mod, b_mod.reshape(DEPTH, 1, width))


def _mod_spec(layer, tm, t, b_off):
    g = max(1, tm // t)
    tiles_per_stream = max(1, t // tm)
    assert b_off % g == 0 and (tm % t == 0 or t % tm == 0)
    base = b_off // g
    return pl.BlockSpec((None, g, N_MOD, D_MODEL),
                        lambda i, j: (layer, base + i // tiles_per_stream, 0, 0))


FF_TILE = 512


def _ffn_kernel(x_ref, m_ref, g_ref, w13_ref, w2_ref, fg_ref, o_ref, h_ref, *, sub, final):
    j = pl.program_id(1)
    last = pl.num_programs(1) - 1
    tm = x_ref.shape[0]

    def gated(h):
        ab = _dot(h, w13_ref[...])
        a = ab[:, :FF_TILE]
        return (a * jax.nn.sigmoid(a) * ab[:, FF_TILE:]).astype(BF16)

    @pl.when(j == 0)
    def _():
        store_norm, block = _modulated_norm_blocks(h_ref, x_ref, g_ref, m_ref, sub)
        store_norm(0)
        for r in range(ROW_SPLIT):
            if r + 1 < ROW_SPLIT:
                store_norm(r + 1)
            rows = slice(r * block, (r + 1) * block)
            o_ref[rows, :] = _dot(gated(h_ref[rows, :]), w2_ref[...])

    @pl.when(jnp.logical_and(j > 0, j < last))
    def _():
        o_ref[...] += _dot(gated(h_ref[...]), w2_ref[...])

    @pl.when(j == last)
    def _():
        block = tm // ROW_SPLIT
        u = gated(h_ref[...])
        gate = 0.5 * _rows(m_ref[...][:, 3 * sub + 2, :], tm)
        for r in range(ROW_SPLIT):
            rows = slice(r * block, (r + 1) * block)
            acc = o_ref[rows, :] + _dot(u[rows, :], w2_ref[...])
            y = x_ref[rows, :] + (gate if gate.shape[0] == 1 else gate[rows, :]) * acc
            if final:
                y = y * lax.rsqrt(jnp.mean(y * y, axis=-1, keepdims=True) + RMS_EPS) * fg_ref[...]
            o_ref[rows, :] = y


def _ffn(x, mod, norm_g, w13, w2, final_g, *, layer, which, sub, t, b_off, tm, final):
    n = x.shape[0]
    tf = FF_TILE
    assert D_FF // tf >= 2
    kern = functools.partial(_ffn_kernel, sub=sub, final=final)
    return pl.pallas_call(
        kern,
        grid=(n // tm, D_FF // tf),
        in_specs=[
            pl.BlockSpec((tm, D_MODEL), lambda i, j: (i, 0)),
            _mod_spec(layer, tm, t, b_off),
            pl.BlockSpec((None, None, 1, D_MODEL), lambda i, j: (layer, sub, 0, 0)),
            pl.BlockSpec((None, None, D_MODEL, 2 * tf), lambda i, j: (layer, which, 0, j)),
            pl.BlockSpec((None, None, tf, D_MODEL), lambda i, j: (layer, which, j, 0)),
            pl.BlockSpec((1, D_MODEL), lambda i, j: (0, 0)),
        ],
        out_specs=pl.BlockSpec((tm, D_MODEL), lambda i, j: (i, 0)),
        out_shape=jax.ShapeDtypeStruct((n, D_MODEL), F32),
        scratch_shapes=[pltpu.VMEM((tm, D_MODEL), BF16)],
        compiler_params=pltpu.CompilerParams(
            dimension_semantics=("arbitrary", "arbitrary"), vmem_limit_bytes=FFN_VMEM_LIMIT),
        name="ffn",
    )(x, mod, norm_g, w13, w2, final_g)


def _proj_kernel(x_ref, m_ref, g_ref, w_ref, o_ref, h_ref):
    j = pl.program_id(1)

    @pl.when(j == 0)
    def _():
        store_norm, block = _modulated_norm_blocks(h_ref, x_ref, g_ref, m_ref, 1)
        store_norm(0)
        for r in range(ROW_SPLIT):
            if r + 1 < ROW_SPLIT:
                store_norm(r + 1)
            rows = slice(r * block, (r + 1) * block)
            o_ref[rows, :] = _dot(h_ref[rows, :], w_ref[...])

    @pl.when(j > 0)
    def _():
        o_ref[...] = _dot(h_ref[...], w_ref[...])


def _in_proj(x, mod, norm_g, w_in, *, layer, t, b_off, tm):
    n = x.shape[0]
    return pl.pallas_call(
        _proj_kernel,
        grid=(n // tm, PROJ_WIDTH // PROJ_TILE),
        in_specs=[
            pl.BlockSpec((tm, D_MODEL), lambda i, j: (i, 0)),
            _mod_spec(layer, tm, t, b_off),
            pl.BlockSpec((None, None, 1, D_MODEL), lambda i, j: (layer, 1, 0, 0)),
            pl.BlockSpec((None, D_MODEL, PROJ_TILE), lambda i, j: (layer, 0, j)),
        ],
        out_specs=pl.BlockSpec((tm, PROJ_TILE), lambda i, j: (i, j)),
        out_shape=jax.ShapeDtypeStruct((n, PROJ_WIDTH), F32),
        scratch_shapes=[pltpu.VMEM((tm, D_MODEL), BF16)],
        compiler_params=pltpu.CompilerParams(
            dimension_semantics=("arbitrary", "arbitrary"), vmem_limit_bytes=VMEM_LIMIT),
        name="in_proj",
    )(x, mod, norm_g, w_in)


def _split3(x):
    hi = x.astype(BF16)
    r1 = x - hi.astype(F32)
    mid = r1.astype(BF16)
    lo = (r1 - mid.astype(F32)).astype(BF16)
    return hi, mid, lo


def _wkv_kernel(p_ref, sh0_ref, s0_ref, mu_ref, vec_ref, wlr_ref, o_ref, sout_ref, state_ref, prev_ref, *, chunk):
    c = pl.program_id(1)
    C = chunk
    C2 = 2 * C
    n_streams = p_ref.shape[1]

    @pl.when(c == 0)
    def _():
        state_ref[...] = s0_ref[0]
        prev_ref[...] = sh0_ref[0]

    w0, a0, k_k, k_a, lnx_w, lnx_b, r_k = (vec_ref[i:i + 1, :] for i in range(7))
    lane = lax.broadcasted_iota(jnp.int32, (1, LANES), 1)
    head0 = lane < HEAD_SIZE
    ri = lax.broadcasted_iota(jnp.int32, (LANES, LANES), 0)
    ci = lax.broadcasted_iota(jnp.int32, (LANES, LANES), 1)
    head_ones = ((ri < HEAD_SIZE) == (ci < HEAD_SIZE)).astype(BF16)
    tr = lax.broadcasted_iota(jnp.int32, (C, C), 0)
    tc = lax.broadcasted_iota(jnp.int32, (C, C), 1)
    cum_ones = (tc <= tr).astype(BF16)
    row = lax.broadcasted_iota(jnp.int32, (C, 1), 0)

    def head_sum(x):
        st = jnp.concatenate([x[:, g * LANES:(g + 1) * LANES] for g in range(N_PAIRS)], axis=0)
        s = _dot(st.astype(BF16), head_ones)
        return jnp.concatenate([s[g * C:(g + 1) * C] for g in range(N_PAIRS)], axis=1)

    def prepare(s):
        p = p_ref[0, s]
        prev = jnp.where(row == 0, prev_ref[s], pltpu.roll(p, 1, 0))
        prev_ref[s] = p[C - 1:C, :]
        xs = p + (prev - p) * mu_ref[...]

        r = xs[:, 0:RWKV_WIDTH]
        k = xs[:, RWKV_WIDTH:2 * RWKV_WIDTH]
        v = xs[:, 2 * RWKV_WIDTH:3 * RWKV_WIDTH]
        z = xs[:, 3 * RWKV_WIDTH:RWKV_COLS]
        zc = lax.broadcasted_iota(jnp.int32, z.shape, 1)
        zact = jnp.where(zc < DECAY_RANK, jnp.tanh(z),
                         jnp.where(zc < DECAY_RANK + A_RANK, z, jax.nn.sigmoid(z)))
        lr = _dot(zact.astype(BF16), wlr_ref[...])

        zz = -(w0 + lr[:, 0:RWKV_WIDTH])
        softplus = jnp.maximum(zz, 0.0) + jnp.log(1.0 + jnp.exp(-jnp.abs(zz)))
        lw = -jnp.exp(-softplus - 0.5)
        a = jax.nn.sigmoid(a0 + lr[:, RWKV_WIDTH:2 * RWKV_WIDTH])

        hi, mid, lo = _split3(lw)
        cs = _dot(cum_ones, hi) + _dot(cum_ones, mid) + _dot(cum_ones, lo)
        w_in = jnp.exp(cs)
        w_inv = jnp.exp(-cs)
        w_ex = jnp.exp(cs - lw)

        kk = k * k_k
        k2 = k * (1.0 + (a - 1.0) * k_a)
        kk = kk * lax.rsqrt(jnp.maximum(head_sum(kk * kk), 1e-24))
        b = kk * a
        k_h = k2 * w_inv
        b_h = b * w_inv
        w_c = w_in[C - 1:C, :]
        return dict(v=v, r_t=r * w_in, kk_t=kk * w_ex, k_h=k_h, b_h=b_h, w_c=w_c, k_b=k_h * w_c, b_b=b_h * w_c,
                    gate=lr[:, 2 * RWKV_WIDTH:3 * RWKV_WIDTH], bonus=head_sum(r * k2 * r_k) * v)

    pre = [prepare(s) for s in range(n_streams)]

    tr2 = lax.broadcasted_iota(jnp.int32, (C, C2), 0)
    tc2 = lax.broadcasted_iota(jnp.int32, (C, C2), 1)
    col_head0 = tc2 < C
    ts2 = jnp.where(col_head0, tc2, tc2 - C)
    strict = ts2 < tr2
    incl = ts2 <= tr2

    def stack(x):
        xb = x.astype(BF16)
        zero = jnp.zeros_like(xb)
        return jnp.concatenate([jnp.where(head0, xb, zero), jnp.where(head0, zero, xb)], axis=0)

    def block_diag(p):
        pb = p.astype(BF16)
        zero = jnp.zeros_like(pb)
        return jnp.concatenate([jnp.where(col_head0, pb, zero), jnp.where(col_head0, zero, pb)], axis=0)

    chains = [(s, g) for s in range(n_streams) for g in range(N_PAIRS)]
    pairs = range(len(chains))

    def lanes(name, i):
        s, g = chains[i]
        return pre[s][name][:, g * LANES:(g + 1) * LANES]

    s_prev = [state_ref[s, g] for s, g in chains]
    q = [jnp.concatenate([lanes("kk_t", i), lanes("r_t", i)], axis=0).astype(BF16) for i in pairs]
    kb = [jnp.concatenate([stack(lanes("k_h", i)), stack(lanes("b_h", i))], axis=0) for i in pairs]
    s_v = [stack(lanes("v", i)) for i in pairs]
    kb_c = [jnp.concatenate([stack(lanes("k_b", i)), -stack(lanes("b_b", i))], axis=0) for i in pairs]

    scores = [_dot_nt(q[g], kb[g]) for g in pairs]
    proj = [_dot_nt(q[g], s_prev[g].astype(BF16)) for g in pairs]
    a_kk_k = [jnp.where(strict, s[:C, :C2], 0.0).astype(BF16) for s in scores]
    m_b = [jnp.where(strict, s[:C, C2:], 0.0) for s in scores]
    a_r = [jnp.concatenate([jnp.where(incl, s[C:, :C2], 0.0), jnp.where(incl, -s[C:, C2:], 0.0)],
                           axis=1).astype(BF16) for s in scores]
    z = [_dot(a_kk_k[g], s_v[g]) + proj[g][:C] for g in pairs]

    pw = m_b
    sign = -1.0
    n = 1
    while n < C:
        last = 2 * n >= C
        nxt = []
        for g in pairs:
            if last:
                pz = _dot(pw[g].astype(BF16), stack(z[g]))
            else:
                both = _dot(pw[g].astype(BF16), jnp.concatenate([block_diag(pw[g]), stack(z[g])], axis=1))
                nxt.append(both[:, :C2])
                pz = both[:, C2:]
            z[g] = z[g] + sign * pz
        pw = nxt
        sign = 1.0
        n *= 2

    v_sa = [jnp.concatenate([s_v[g], stack(z[g])], axis=0) for g in pairs]
    ys = []
    for i in pairs:
        s, g = chains[i]
        ys.append(proj[i][C:] + _dot(a_r[i], v_sa[i]))
        state_ref[s, g] = s_prev[i] * lanes("w_c", i) + _dot_tn(v_sa[i], kb_c[i])

    for s in range(n_streams):
        y = jnp.concatenate(ys[s * N_PAIRS:(s + 1) * N_PAIRS], axis=1)
        mean = head_sum(y) * (1.0 / HEAD_SIZE)
        d = y - mean
        var = head_sum(d * d) * (1.0 / HEAD_SIZE)
        yn = d * lax.rsqrt(var + GN_EPS) * lnx_w + lnx_b + pre[s]["bonus"]
        o_ref[0, s] = (yn * pre[s]["gate"]).astype(BF16)

    @pl.when(c == pl.num_programs(1) - 1)
    def _():
        sout_ref[0] = state_ref[...]


WKV_STREAMS = 2


def _wkv(proj, shift0, s0_pairs, mu, vecs, w_lowrank, *, layer, nb, t, chunk):
    ns = WKV_STREAMS
    nc = t // chunk
    kern = functools.partial(_wkv_kernel, chunk=chunk)
    grouped = lambda a: a.reshape((nb // ns, ns) + a.shape[1:])
    yg, s_new = pl.pallas_call(
        kern,
        grid=(nb // ns, nc),
        in_specs=[
            pl.BlockSpec((1, ns, chunk, RWKV_COLS), lambda b, c: (b, 0, c, 0)),
            pl.BlockSpec((1, ns, 1, RWKV_COLS), lambda b, c: (b, 0, 0, 0)),
            pl.BlockSpec((1, ns, N_PAIRS, LANES, LANES), lambda b, c: (b, 0, 0, 0, 0)),
            pl.BlockSpec((None, 1, RWKV_COLS), lambda b, c: (layer, 0, 0)),
            pl.BlockSpec((None, 8, RWKV_WIDTH), lambda b, c: (layer, 0, 0)),
            pl.BlockSpec((None, LOWRANK_PAD, 3 * RWKV_WIDTH), lambda b, c: (layer, 0, 0)),
        ],
        out_specs=[
            pl.BlockSpec((1, ns, chunk, RWKV_WIDTH), lambda b, c: (b, 0, c, 0)),
            pl.BlockSpec((1, ns, N_PAIRS, LANES, LANES), lambda b, c: (b, 0, 0, 0, 0)),
        ],
        out_shape=[
            jax.ShapeDtypeStruct((nb // ns, ns, t, RWKV_WIDTH), BF16),
            jax.ShapeDtypeStruct((nb // ns, ns, N_PAIRS, LANES, LANES), F32),
        ],
        scratch_shapes=[pltpu.VMEM((ns, N_PAIRS, LANES, LANES), F32), pltpu.VMEM((ns, 1, RWKV_COLS), F32)],
        compiler_params=pltpu.CompilerParams(
            dimension_semantics=("arbitrary", "arbitrary"), vmem_limit_bytes=VMEM_LIMIT),
        name="wkv",
    )(proj.reshape(nb // ns, ns, t, PROJ_WIDTH), grouped(shift0), grouped(s0_pairs), mu, vecs, w_lowrank)
    return yg.reshape(nb * t, RWKV_WIDTH), s_new.reshape(nb, N_PAIRS, LANES, LANES)


def _merge_kernel(x_ref, m_ref, yg_ref, pp_ref, buf_ref, gt_ref, wo_ref, wp_ref, ps_ref, wout_ref, o_ref, carry_ref,
                  *, pos0, tm):
    i = pl.program_id(1)

    @pl.when(i == 0)
    def _():
        carry_ref[...] = buf_ref[0]

    xp = pp_ref[...]
    full = jnp.concatenate([carry_ref[...], xp], axis=0)
    carry_ref[...] = full[tm:tm + POOL_CARRY]

    pos = pos0 + i * tm + lax.broadcasted_iota(jnp.int32, (tm, POOL_GROUP), 0)
    o_b = []
    for gi, win in enumerate(POOL_WINDOWS):
        cols = slice(gi * POOL_GROUP, (gi + 1) * POOL_GROUP)
        s = full[:, cols]
        width = 1
        while width < win:
            s = s[width:] + s[:-width]
            width *= 2
        s = s[POOL_CARRY - (win - 1):]
        cnt = jnp.minimum(win, pos + 1).astype(F32)
        pooled = s / cnt - xp[:, cols]
        o_b.append(_dot(pooled.astype(BF16), wp_ref[gi]))
    o_b = jnp.concatenate(o_b, axis=1) * ps_ref[...]
    o_a = _dot(yg_ref[...], wo_ref[...])
    gates = jax.nn.sigmoid(gt_ref[...])
    merged = gates[:, :D_MODEL] * o_a + gates[:, D_MODEL:] * o_b
    o_ref[...] = x_ref[...] + m_ref[...][:, 5, :] * _dot(merged.astype(BF16), wout_ref[...])


def _merge(x, mod, yg, proj, pool0, w_o, w_pool, pool_scale, w_out, *, layer, nb, t, b_off, tm, pos0):
    n = x.shape[0]
    nt = t // tm
    kern = functools.partial(_merge_kernel, pos0=pos0, tm=tm)
    row = lambda b, i: (b * nt + i, 0)
    return pl.pallas_call(
        kern,
        grid=(nb, nt),
        in_specs=[
            pl.BlockSpec((tm, D_MODEL), row),
            pl.BlockSpec((None, 1, N_MOD, D_MODEL), lambda b, i: (layer, b_off + b, 0, 0)),
            pl.BlockSpec((tm, RWKV_WIDTH), row),
            pl.BlockSpec((tm, POOL_WIDTH), lambda b, i: (b * nt + i, POOL_COL // POOL_WIDTH)),
            pl.BlockSpec((1, POOL_CARRY, POOL_WIDTH), lambda b, i: (b, 0, 0)),
            pl.BlockSpec((tm, 2 * D_MODEL), lambda b, i: (b * nt + i, GATE_COL // (2 * D_MODEL))),
            pl.BlockSpec((None, RWKV_WIDTH, D_MODEL), lambda b, i: (layer, 0, 0)),
            pl.BlockSpec((None, len(POOL_WINDOWS), POOL_GROUP, POOL_OUT_GROUP), lambda b, i: (layer, 0, 0, 0)),
            pl.BlockSpec((None, 1, D_MODEL), lambda b, i: (layer, 0, 0)),
            pl.BlockSpec((None, D_MODEL, D_MODEL), lambda b, i: (layer, 0, 0)),
        ],
        out_specs=pl.BlockSpec((tm, D_MODEL), row),
        out_shape=jax.ShapeDtypeStruct((n, D_MODEL), F32),
        scratch_shapes=[pltpu.VMEM((POOL_CARRY, POOL_WIDTH), F32)],
        compiler_params=pltpu.CompilerParams(
            dimension_semantics=("arbitrary", "arbitrary"), vmem_limit_bytes=VMEM_LIMIT),
        name="merge",
    )(x, mod, yg, proj, pool0, proj, w_o, w_pool, pool_scale, w_out)


def _to_pairs(s):
    b = s.shape[0]
    s = s.reshape(b, N_PAIRS, 2, HEAD_SIZE, HEAD_SIZE)
    z = jnp.zeros_like(s[:, :, 0])
    top = jnp.concatenate([s[:, :, 0], z], axis=-1)
    bot = jnp.concatenate([z, s[:, :, 1]], axis=-1)
    return jnp.concatenate([top, bot], axis=-2)


def _from_pairs(sp):
    b = sp.shape[0]
    h0 = sp[:, :, :HEAD_SIZE, :HEAD_SIZE]
    h1 = sp[:, :, HEAD_SIZE:, HEAD_SIZE:]
    return jnp.stack([h0, h1], axis=2).reshape(b, N_HEADS, HEAD_SIZE, HEAD_SIZE)


def _stream_set(x, t, nb, b_off, *, ffn_tm, proj_tm, merge_tm, chunk, pos0):
    return dict(x=x.reshape(nb * t, D_MODEL), t=t, nb=nb, b_off=b_off, ffn_tm=ffn_tm, proj_tm=proj_tm,
                merge_tm=merge_tm, chunk=chunk, pos0=pos0)


def kernel(x_prompt, x_sample, c_prompt, c_sample, state_wkv, state_shift, state_pool, norm_g, w_mod, b_mod,
           ffn_w1, ffn_w3, ffn_w2, w_in, shift_mu, w0, w_decay, a0, w_a, w_gate, k_k, k_a, r_k, lnx_w, lnx_b,
           w_o_rwkv, w_pool, pool_scale, w_out, final_g):
    bp, seq, _ = x_prompt.shape
    bs, dseq, _ = x_sample.shape
    past_len = 1024
    dt = x_prompt.dtype

    tiled = lambda w: w.astype(BF16).reshape(DEPTH, 2, D_MODEL, D_FF // FF_TILE, FF_TILE)
    w13 = jnp.concatenate([tiled(ffn_w1), tiled(ffn_w3)], axis=-1).reshape(DEPTH, 2, D_MODEL, 2 * D_FF)
    w2 = ffn_w2.astype(BF16)
    pad = jnp.zeros((DEPTH, D_MODEL, GATE_COL - SHIFT_WIDTH), BF16)
    w_in_p = jnp.concatenate([w_in[:, :, :SHIFT_WIDTH].astype(BF16), pad,
                              w_in[:, :, SHIFT_WIDTH + POOL_WIDTH:].astype(BF16),
                              w_in[:, :, SHIFT_WIDTH:SHIFT_WIDTH + POOL_WIDTH].astype(BF16)],
                             axis=2)
    w_o_b = w_o_rwkv.astype(BF16)
    w_pool_b = w_pool.astype(BF16)
    w_out_b = w_out.astype(BF16)
    zr = lambda rows: jnp.zeros((DEPTH, rows, RWKV_WIDTH), F32)
    w_lr = jnp.concatenate([
        jnp.concatenate([w_decay, zr(LOWRANK_PAD - DECAY_RANK)], axis=1),
        jnp.concatenate([zr(DECAY_RANK), w_a, zr(LOWRANK_PAD - DECAY_RANK - A_RANK)], axis=1),
        jnp.concatenate([zr(DECAY_RANK + A_RANK), w_gate, zr(LOWRANK_PAD - LOWRANK)], axis=1),
    ], axis=2).astype(BF16)
    vecs = jnp.stack([w0, a0, k_k, k_a, lnx_w, lnx_b, r_k.reshape(DEPTH, RWKV_WIDTH),
                      jnp.zeros_like(w0)], axis=1)
    mu = jnp.pad(shift_mu, ((0, 0), (0, RWKV_COLS - SHIFT_WIDTH))).reshape(DEPTH, 1, RWKV_COLS)
    norm_g4 = norm_g.reshape(DEPTH, 3, 1, D_MODEL)
    fg = final_g.reshape(1, D_MODEL)
    ps = pool_scale.reshape(DEPTH, 1, D_MODEL)

    mod = _modulation(jnp.concatenate([c_prompt, c_sample], axis=0), w_mod, b_mod)
    mod = mod.reshape(DEPTH, bp + bs, N_MOD, D_MODEL)

    prompt = _stream_set(x_prompt, seq, bp, 0, ffn_tm=min(1024, seq), proj_tm=min(1024, seq),
                         merge_tm=min(256, seq), chunk=min(64, seq), pos0=0)
    sample = _stream_set(x_sample, dseq, bs, bp, ffn_tm=bs * dseq, proj_tm=bs * dseq, merge_tm=dseq,
                         chunk=dseq, pos0=past_len)

    outs = {"p": ([], [], []), "s": ([], [], [])}
    for l in range(DEPTH):
        for tag, st in (("p", prompt), ("s", sample)):
            nb, t = st["nb"], st["t"]
            if tag == "p":
                shift0 = jnp.zeros((nb, 1, RWKV_COLS), F32)
                s0 = jnp.zeros((nb, N_PAIRS, LANES, LANES), F32)
                pool0 = jnp.zeros((nb, POOL_CARRY, POOL_WIDTH), F32)
            else:
                shift0 = jnp.pad(state_shift[l], ((0, 0), (0, RWKV_COLS - SHIFT_WIDTH))).reshape(nb, 1, RWKV_COLS)
                s0 = _to_pairs(state_wkv[l].astype(F32))
                pool0 = jnp.pad(state_pool[l], ((0, 0), (POOL_CARRY - POOL_BUF, 0), (0, 0)))
            common = dict(layer=l, t=t, b_off=st["b_off"])
            x = st["x"]
            x = _ffn(x, mod, norm_g4, w13, w2, fg, which=0, sub=0, tm=st["ffn_tm"], final=False, **common)
            proj = _in_proj(x, mod, norm_g4, w_in_p, tm=st["proj_tm"], **common)
            yg, s_new = _wkv(proj, shift0, s0, mu, vecs, w_lr, layer=l, nb=nb, t=t, chunk=st["chunk"])
            x = _merge(x, mod, yg, proj, pool0, w_o_b, w_pool_b, ps, w_out_b, nb=nb,
                       tm=st["merge_tm"], pos0=st["pos0"], **common)
            x = _ffn(x, mod, norm_g4, w13, w2, fg, which=1, sub=2, tm=st["ffn_tm"],
                     final=(l == DEPTH - 1), **common)
            st["x"] = x
            wkv_l, shift_l, pool_l = outs[tag]
            wkv_l.append(_from_pairs(s_new).astype(dt))
            proj3 = proj.reshape(nb, t, PROJ_WIDTH)
            shift_l.append(proj3[:, -1, :SHIFT_WIDTH])
            pool_l.append(proj3[:, t - POOL_BUF:, POOL_COL:])

    yp = prompt["x"].reshape(bp, seq, D_MODEL)
    ys = sample["x"].reshape(bs, dseq, D_MODEL)
    pw, psh, ppl = (jnp.stack(v) for v in outs["p"])
    sw, ssh, spl = (jnp.stack(v) for v in outs["s"])
    return (yp, ys, pw, psh, ppl, sw, ssh, spl)
```

```python
import functools

import jax
import jax.numpy as jnp
from jax import lax
from jax.experimental import pallas as pl
from jax.experimental.pallas import tpu as pltpu

F32 = jnp.float32
BF16 = jnp.bfloat16

D_MODEL = 2048
DEPTH = 4
RWKV_WIDTH = D_MODEL // 2
HEAD_SIZE = 64
N_HEADS = RWKV_WIDTH // HEAD_SIZE
DECAY_RANK = 64
A_RANK = 64
GATE_RANK = 160
POOL_WIDTH = D_MODEL // 2
POOL_WINDOWS = (2, 4, 8, 16)
POOL_GROUP = POOL_WIDTH // len(POOL_WINDOWS)
POOL_OUT_GROUP = D_MODEL // len(POOL_WINDOWS)
POOL_BUF = max(POOL_WINDOWS) - 1
D_FF = ((8 * D_MODEL // 3 + 255) // 256) * 256
SHIFT_WIDTH = 3 * RWKV_WIDTH + DECAY_RANK + A_RANK + GATE_RANK
N_MOD = 9
RMS_EPS = 1e-6
GN_EPS = 64e-5

LANES = 128
N_PAIRS = RWKV_WIDTH // LANES
LOWRANK = DECAY_RANK + A_RANK + GATE_RANK
LOWRANK_PAD = 3 * LANES
RWKV_COLS = 3 * RWKV_WIDTH + LOWRANK_PAD
PROJ_TILE = 1024
GATE_COL = 2 * D_MODEL
POOL_COL = 4 * D_MODEL
PROJ_WIDTH = POOL_COL + POOL_WIDTH
POOL_CARRY = 16
VMEM_LIMIT = 56 * 1024 * 1024
WIDE_VMEM_LIMIT = 62 * 1024 * 1024


def _dot(a, b):
    return jnp.dot(a, b, preferred_element_type=F32)


def _dot_nt(a, b):
    return lax.dot_general(a, b, (((1,), (1,)), ((), ())), preferred_element_type=F32)


def _dot_tn(a, b):
    return lax.dot_general(a, b, (((0,), (0,)), ((), ())), preferred_element_type=F32)


def _rows(v, tm):
    g, d = v.shape
    if g == 1:
        return v
    return jnp.broadcast_to(v[:, None, :], (g, tm // g, d)).reshape(tm, d)


NORM_ROWS = 16


ROW_SPLIT = 4


def _modulated_norm_blocks(h_ref, x_ref, g_ref, m_ref, sub):
    tm = x_ref.shape[0]
    rows_per_stream = tm // m_ref.shape[0]
    block = tm // ROW_SPLIT
    gain = g_ref[...]
    cache = {}

    def scale_shift(s):
        if s not in cache:
            cache[s] = (gain * (1.0 + m_ref[s, 3 * sub + 1:3 * sub + 2, :]), m_ref[s, 3 * sub:3 * sub + 1, :])
        return cache[s]

    def store(r):
        for r0 in range(r * block, (r + 1) * block, NORM_ROWS):
            gs, shift = scale_shift(r0 // rows_per_stream)
            x = x_ref[r0:r0 + NORM_ROWS, :]
            y = x * lax.rsqrt(jnp.mean(x * x, axis=-1, keepdims=True) + RMS_EPS)
            h_ref[r0:r0 + NORM_ROWS, :] = (y * gs + shift).astype(BF16)

    return store, block


def _mod_kernel(c_ref, w_ref, b_ref, o_ref):
    c = c_ref[...]
    s = (c * jax.nn.sigmoid(c)).astype(BF16)
    o_ref[0] = _dot(s, w_ref[0].astype(BF16)) + b_ref[0]


def _modulation(c_all, w_mod, b_mod):
    nb = c_all.shape[0]
    width = N_MOD * D_MODEL
    tn = 1024
    return pl.pallas_call(
        _mod_kernel,
        grid=(DEPTH, width // tn),
        in_specs=[
            pl.BlockSpec((nb, D_MODEL), lambda l, j: (0, 0)),
            pl.BlockSpec((1, D_MODEL, tn), lambda l, j: (l, 0, j)),
            pl.BlockSpec((1, 1, tn), lambda l, j: (l, 0, j)),
        ],
        out_specs=pl.BlockSpec((1, nb, tn), lambda l, j: (l, 0, j)),
        out_shape=jax.ShapeDtypeStruct((DEPTH, nb, width), F32),
        compiler_params=pltpu.CompilerParams(
            dimension_semantics=("arbitrary", "arbitrary"), vmem_limit_bytes=VMEM_LIMIT),
        name="modulation",
    )(c_all, w_mod, b_mod.reshape(DEPTH, 1, width))


def _mod_spec(layer, tm, t, b_off):
    g = max(1, tm // t)
    tiles_per_stream = max(1, t // tm)
    assert b_off % g == 0 and (tm % t == 0 or t % tm == 0)
    base = b_off // g
    return pl.BlockSpec((None, g, N_MOD, D_MODEL),
                        lambda i, j: (layer, base + i // tiles_per_stream, 0, 0))


FF_TILE = 512


def _ffn_kernel(x_ref, m_ref, g_ref, w1_ref, w3_ref, w2_ref, fg_ref, o_ref, h_ref, *, sub, final):
    j = pl.program_id(1)
    last = pl.num_programs(1) - 1
    tm = x_ref.shape[0]

    def gated(h):
        a = _dot(h, w1_ref[...])
        return (a * jax.nn.sigmoid(a) * _dot(h, w3_ref[...])).astype(BF16)

    @pl.when(j == 0)
    def _():
        store_norm, block = _modulated_norm_blocks(h_ref, x_ref, g_ref, m_ref, sub)
        store_norm(0)
        for r in range(ROW_SPLIT):
            if r + 1 < ROW_SPLIT:
                store_norm(r + 1)
            rows = slice(r * block, (r + 1) * block)
            o_ref[rows, :] = _dot(gated(h_ref[rows, :]), w2_ref[...])

    @pl.when(jnp.logical_and(j > 0, j < last))
    def _():
        o_ref[...] += _dot(gated(h_ref[...]), w2_ref[...])

    @pl.when(j == last)
    def _():
        block = tm // ROW_SPLIT
        u = gated(h_ref[...])
        gate = 0.5 * _rows(m_ref[...][:, 3 * sub + 2, :], tm)
        for r in range(ROW_SPLIT):
            rows = slice(r * block, (r + 1) * block)
            acc = o_ref[rows, :] + _dot(u[rows, :], w2_ref[...])
            y = x_ref[rows, :] + (gate if gate.shape[0] == 1 else gate[rows, :]) * acc
            if final:
                y = y * lax.rsqrt(jnp.mean(y * y, axis=-1, keepdims=True) + RMS_EPS) * fg_ref[...]
            o_ref[rows, :] = y


def _ffn(x, mod, norm_g, w1, w3, w2, final_g, *, layer, which, sub, t, b_off, tm, final):
    n = x.shape[0]
    tf = FF_TILE
    assert D_FF // tf >= 2
    kern = functools.partial(_ffn_kernel, sub=sub, final=final)
    return pl.pallas_call(
        kern,
        grid=(n // tm, D_FF // tf),
        in_specs=[
            pl.BlockSpec((tm, D_MODEL), lambda i, j: (i, 0)),
            _mod_spec(layer, tm, t, b_off),
            pl.BlockSpec((None, None, 1, D_MODEL), lambda i, j: (layer, sub, 0, 0)),
            pl.BlockSpec((None, None, D_MODEL, tf), lambda i, j: (layer, which, 0, j)),
            pl.BlockSpec((None, None, D_MODEL, tf), lambda i, j: (layer, which, 0, j)),
            pl.BlockSpec((None, None, tf, D_MODEL), lambda i, j: (layer, which, j, 0)),
            pl.BlockSpec((1, D_MODEL), lambda i, j: (0, 0)),
        ],
        out_specs=pl.BlockSpec((tm, D_MODEL), lambda i, j: (i, 0)),
        out_shape=jax.ShapeDtypeStruct((n, D_MODEL), F32),
        scratch_shapes=[pltpu.VMEM((tm, D_MODEL), BF16)],
        compiler_params=pltpu.CompilerParams(
            dimension_semantics=("arbitrary", "arbitrary"), vmem_limit_bytes=WIDE_VMEM_LIMIT),
        name="ffn",
    )(x, mod, norm_g, w1, w3, w2, final_g)


def _proj_kernel(x_ref, m_ref, g_ref, w_ref, o_ref, h_ref):
    j = pl.program_id(1)

    @pl.when(j == 0)
    def _():
        store_norm, block = _modulated_norm_blocks(h_ref, x_ref, g_ref, m_ref, 1)
        store_norm(0)
        for r in range(ROW_SPLIT):
            if r + 1 < ROW_SPLIT:
                store_norm(r + 1)
            rows = slice(r * block, (r + 1) * block)
            o_ref[rows, :] = _dot(h_ref[rows, :], w_ref[...])

    @pl.when(j > 0)
    def _():
        o_ref[...] = _dot(h_ref[...], w_ref[...])


def _in_proj(x, mod, norm_g, w_in, *, layer, t, b_off, tm):
    n = x.shape[0]
    return pl.pallas_call(
        _proj_kernel,
        grid=(n // tm, PROJ_WIDTH // PROJ_TILE),
        in_specs=[
            pl.BlockSpec((tm, D_MODEL), lambda i, j: (i, 0)),
            _mod_spec(layer, tm, t, b_off),
            pl.BlockSpec((None, None, 1, D_MODEL), lambda i, j: (layer, 1, 0, 0)),
            pl.BlockSpec((None, D_MODEL, PROJ_TILE), lambda i, j: (layer, 0, j)),
        ],
        out_specs=pl.BlockSpec((tm, PROJ_TILE), lambda i, j: (i, j)),
        out_shape=jax.ShapeDtypeStruct((n, PROJ_WIDTH), F32),
        scratch_shapes=[pltpu.VMEM((tm, D_MODEL), BF16)],
        compiler_params=pltpu.CompilerParams(
            dimension_semantics=("arbitrary", "arbitrary"), vmem_limit_bytes=VMEM_LIMIT),
        name="in_proj",
    )(x, mod, norm_g, w_in)


def _split3(x):
    hi = x.astype(BF16)
    r1 = x - hi.astype(F32)
    mid = r1.astype(BF16)
    lo = (r1 - mid.astype(F32)).astype(BF16)
    return hi, mid, lo


def _wkv_kernel(p_ref, sh0_ref, s0_ref, mu_ref, vec_ref, wlr_ref, o_ref, sout_ref, state_ref, prev_ref, *, chunk):
    c = pl.program_id(1)
    C = chunk
    C2 = 2 * C
    n_streams = p_ref.shape[1]

    @pl.when(c == 0)
    def _():
        state_ref[...] = s0_ref[0]
        prev_ref[...] = sh0_ref[0]

    w0, a0, k_k, k_a, lnx_w, lnx_b, r_k = (vec_ref[i:i + 1, :] for i in range(7))
    lane = lax.broadcasted_iota(jnp.int32, (1, LANES), 1)
    head0 = lane < HEAD_SIZE
    ri = lax.broadcasted_iota(jnp.int32, (LANES, LANES), 0)
    ci = lax.broadcasted_iota(jnp.int32, (LANES, LANES), 1)
    head_ones = ((ri < HEAD_SIZE) == (ci < HEAD_SIZE)).astype(BF16)
    tr = lax.broadcasted_iota(jnp.int32, (C, C), 0)
    tc = lax.broadcasted_iota(jnp.int32, (C, C), 1)
    cum_ones = (tc <= tr).astype(BF16)
    row = lax.broadcasted_iota(jnp.int32, (C, 1), 0)

    def head_sum(x):
        st = jnp.concatenate([x[:, g * LANES:(g + 1) * LANES] for g in range(N_PAIRS)], axis=0)
        s = _dot(st.astype(BF16), head_ones)
        return jnp.concatenate([s[g * C:(g + 1) * C] for g in range(N_PAIRS)], axis=1)

    def prepare(s):
        p = p_ref[0, s]
        prev = jnp.where(row == 0, prev_ref[s], pltpu.roll(p, 1, 0))
        prev_ref[s] = p[C - 1:C, :]
        xs = p + (prev - p) * mu_ref[...]

        r = xs[:, 0:RWKV_WIDTH]
        k = xs[:, RWKV_WIDTH:2 * RWKV_WIDTH]
        v = xs[:, 2 * RWKV_WIDTH:3 * RWKV_WIDTH]
        z = xs[:, 3 * RWKV_WIDTH:RWKV_COLS]
        zc = lax.broadcasted_iota(jnp.int32, z.shape, 1)
        zact = jnp.where(zc < DECAY_RANK, jnp.tanh(z),
                         jnp.where(zc < DECAY_RANK + A_RANK, z, jax.nn.sigmoid(z)))
        lr = _dot(zact.astype(BF16), wlr_ref[...])

        zz = -(w0 + lr[:, 0:RWKV_WIDTH])
        softplus = jnp.maximum(zz, 0.0) + jnp.log(1.0 + jnp.exp(-jnp.abs(zz)))
        lw = -jnp.exp(-softplus - 0.5)
        a = jax.nn.sigmoid(a0 + lr[:, RWKV_WIDTH:2 * RWKV_WIDTH])

        hi, mid, lo = _split3(lw)
        cs = _dot(cum_ones, hi) + _dot(cum_ones, mid) + _dot(cum_ones, lo)
        w_in = jnp.exp(cs)
        w_inv = jnp.exp(-cs)
        w_ex = jnp.exp(cs - lw)

        kk = k * k_k
        k2 = k * (1.0 + (a - 1.0) * k_a)
        kk = kk * lax.rsqrt(jnp.maximum(head_sum(kk * kk), 1e-24))
        b = kk * a
        k_h = k2 * w_inv
        b_h = b * w_inv
        w_c = w_in[C - 1:C, :]
        return dict(v=v, r_t=r * w_in, kk_t=kk * w_ex, k_h=k_h, b_h=b_h, w_c=w_c, k_b=k_h * w_c, b_b=b_h * w_c,
                    gate=lr[:, 2 * RWKV_WIDTH:3 * RWKV_WIDTH], bonus=head_sum(r * k2 * r_k) * v)

    pre = [prepare(s) for s in range(n_streams)]

    tr2 = lax.broadcasted_iota(jnp.int32, (C, C2), 0)
    tc2 = lax.broadcasted_iota(jnp.int32, (C, C2), 1)
    col_head0 = tc2 < C
    ts2 = jnp.where(col_head0, tc2, tc2 - C)
    strict = ts2 < tr2
    incl = ts2 <= tr2

    def stack(x):
        xb = x.astype(BF16)
        zero = jnp.zeros_like(xb)
        return jnp.concatenate([jnp.where(head0, xb, zero), jnp.where(head0, zero, xb)], axis=0)

    def block_diag(p):
        pb = p.astype(BF16)
        zero = jnp.zeros_like(pb)
        return jnp.concatenate([jnp.where(col_head0, pb, zero), jnp.where(col_head0, zero, pb)], axis=0)

    chains = [(s, g) for s in range(n_streams) for g in range(N_PAIRS)]
    pairs = range(len(chains))

    def lanes(name, i):
        s, g = chains[i]
        return pre[s][name][:, g * LANES:(g + 1) * LANES]

    s_prev = [state_ref[s, g] for s, g in chains]
    q = [jnp.concatenate([lanes("kk_t", i), lanes("r_t", i)], axis=0).astype(BF16) for i in pairs]
    kb = [jnp.concatenate([stack(lanes("k_h", i)), stack(lanes("b_h", i))], axis=0) for i in pairs]
    s_v = [stack(lanes("v", i)) for i in pairs]
    kb_c = [jnp.concatenate([stack(lanes("k_b", i)), -stack(lanes("b_b", i))], axis=0) for i in pairs]

    scores = [_dot_nt(q[g], kb[g]) for g in pairs]
    proj = [_dot_nt(q[g], s_prev[g].astype(BF16)) for g in pairs]
    a_kk_k = [jnp.where(strict, s[:C, :C2], 0.0).astype(BF16) for s in scores]
    m_b = [jnp.where(strict, s[:C, C2:], 0.0) for s in scores]
    a_r = [jnp.concatenate([jnp.where(incl, s[C:, :C2], 0.0), jnp.where(incl, -s[C:, C2:], 0.0)],
                           axis=1).astype(BF16) for s in scores]
    z = [_dot(a_kk_k[g], s_v[g]) + proj[g][:C] for g in pairs]

    pw = m_b
    sign = -1.0
    n = 1
    while n < C:
        last = 2 * n >= C
        nxt = []
        for g in pairs:
            if last:
                pz = _dot(pw[g].astype(BF16), stack(z[g]))
            else:
                both = _dot(pw[g].astype(BF16), jnp.concatenate([block_diag(pw[g]), stack(z[g])], axis=1))
                nxt.append(both[:, :C2])
                pz = both[:, C2:]
            z[g] = z[g] + sign * pz
        pw = nxt
        sign = 1.0
        n *= 2

    v_sa = [jnp.concatenate([s_v[g], stack(z[g])], axis=0) for g in pairs]
    ys = []
    for i in pairs:
        s, g = chains[i]
        ys.append(proj[i][C:] + _dot(a_r[i], v_sa[i]))
        state_ref[s, g] = s_prev[i] * lanes("w_c", i) + _dot_tn(v_sa[i], kb_c[i])

    for s in range(n_streams):
        y = jnp.concatenate(ys[s * N_PAIRS:(s + 1) * N_PAIRS], axis=1)
        mean = head_sum(y) * (1.0 / HEAD_SIZE)
        d = y - mean
        var = head_sum(d * d) * (1.0 / HEAD_SIZE)
        yn = d * lax.rsqrt(var + GN_EPS) * lnx_w + lnx_b + pre[s]["bonus"]
        o_ref[0, s] = (yn * pre[s]["gate"]).astype(BF16)

    @pl.when(c == pl.num_programs(1) - 1)
    def _():
        sout_ref[0] = state_ref[...]


WKV_STREAMS = 2


def _wkv(proj, shift0, s0_pairs, mu, vecs, w_lowrank, *, layer, nb, t, chunk):
    ns = WKV_STREAMS
    nc = t // chunk
    kern = functools.partial(_wkv_kernel, chunk=chunk)
    grouped = lambda a: a.reshape((nb // ns, ns) + a.shape[1:])
    yg, s_new = pl.pallas_call(
        kern,
        grid=(nb // ns, nc),
        in_specs=[
            pl.BlockSpec((1, ns, chunk, RWKV_COLS), lambda b, c: (b, 0, c, 0)),
            pl.BlockSpec((1, ns, 1, RWKV_COLS), lambda b, c: (b, 0, 0, 0)),
            pl.BlockSpec((1, ns, N_PAIRS, LANES, LANES), lambda b, c: (b, 0, 0, 0, 0)),
            pl.BlockSpec((None, 1, RWKV_COLS), lambda b, c: (layer, 0, 0)),
            pl.BlockSpec((None, 8, RWKV_WIDTH), lambda b, c: (layer, 0, 0)),
            pl.BlockSpec((None, LOWRANK_PAD, 3 * RWKV_WIDTH), lambda b, c: (layer, 0, 0)),
        ],
        out_specs=[
            pl.BlockSpec((1, ns, chunk, RWKV_WIDTH), lambda b, c: (b, 0, c, 0)),
            pl.BlockSpec((1, ns, N_PAIRS, LANES, LANES), lambda b, c: (b, 0, 0, 0, 0)),
        ],
        out_shape=[
            jax.ShapeDtypeStruct((nb // ns, ns, t, RWKV_WIDTH), BF16),
            jax.ShapeDtypeStruct((nb // ns, ns, N_PAIRS, LANES, LANES), F32),
        ],
        scratch_shapes=[pltpu.VMEM((ns, N_PAIRS, LANES, LANES), F32), pltpu.VMEM((ns, 1, RWKV_COLS), F32)],
        compiler_params=pltpu.CompilerParams(
            dimension_semantics=("arbitrary", "arbitrary"), vmem_limit_bytes=VMEM_LIMIT),
        name="wkv",
    )(proj.reshape(nb // ns, ns, t, PROJ_WIDTH), grouped(shift0), grouped(s0_pairs), mu, vecs, w_lowrank)
    return yg.reshape(nb * t, RWKV_WIDTH), s_new.reshape(nb, N_PAIRS, LANES, LANES)


def _merge_kernel(x_ref, m_ref, yg_ref, pp_ref, buf_ref, gt_ref, wo_ref, wp_ref, ps_ref, wout_ref, o_ref, carry_ref,
                  *, pos0, tm):
    i = pl.program_id(1)

    @pl.when(i == 0)
    def _():
        carry_ref[...] = buf_ref[0]

    xp = pp_ref[...]
    full = jnp.concatenate([carry_ref[...], xp], axis=0)
    carry_ref[...] = full[tm:tm + POOL_CARRY]

    pos = pos0 + i * tm + lax.broadcasted_iota(jnp.int32, (tm, POOL_GROUP), 0)
    o_b = []
    for gi, win in enumerate(POOL_WINDOWS):
        cols = slice(gi * POOL_GROUP, (gi + 1) * POOL_GROUP)
        s = full[:, cols]
        width = 1
        while width < win:
            s = s[width:] + s[:-width]
            width *= 2
        s = s[POOL_CARRY - (win - 1):]
        cnt = jnp.minimum(win, pos + 1).astype(F32)
        pooled = s / cnt - xp[:, cols]
        o_b.append(_dot(pooled.astype(BF16), wp_ref[gi]))
    o_b = jnp.concatenate(o_b, axis=1) * ps_ref[...]
    o_a = _dot(yg_ref[...], wo_ref[...])
    gates = jax.nn.sigmoid(gt_ref[...])
    merged = gates[:, :D_MODEL] * o_a + gates[:, D_MODEL:] * o_b
    o_ref[...] = x_ref[...] + m_ref[...][:, 5, :] * _dot(merged.astype(BF16), wout_ref[...])


def _merge(x, mod, yg, proj, pool0, w_o, w_pool, pool_scale, w_out, *, layer, nb, t, b_off, tm, pos0):
    n = x.shape[0]
    nt = t // tm
    kern = functools.partial(_merge_kernel, pos0=pos0, tm=tm)
    row = lambda b, i: (b * nt + i, 0)
    return pl.pallas_call(
        kern,
        grid=(nb, nt),
        in_specs=[
            pl.BlockSpec((tm, D_MODEL), row),
            pl.BlockSpec((None, 1, N_MOD, D_MODEL), lambda b, i: (layer, b_off + b, 0, 0)),
            pl.BlockSpec((tm, RWKV_WIDTH), row),
            pl.BlockSpec((tm, POOL_WIDTH), lambda b, i: (b * nt + i, POOL_COL // POOL_WIDTH)),
            pl.BlockSpec((1, POOL_CARRY, POOL_WIDTH), lambda b, i: (b, 0, 0)),
            pl.BlockSpec((tm, 2 * D_MODEL), lambda b, i: (b * nt + i, GATE_COL // (2 * D_MODEL))),
            pl.BlockSpec((None, RWKV_WIDTH, D_MODEL), lambda b, i: (layer, 0, 0), pipeline_mode=pl.Buffered(1)),
            pl.BlockSpec((None, len(POOL_WINDOWS), POOL_GROUP, POOL_OUT_GROUP), lambda b, i: (layer, 0, 0, 0),
                         pipeline_mode=pl.Buffered(1)),
            pl.BlockSpec((None, 1, D_MODEL), lambda b, i: (layer, 0, 0)),
            pl.BlockSpec((None, D_MODEL, D_MODEL), lambda b, i: (layer, 0, 0), pipeline_mode=pl.Buffered(1)),
        ],
        out_specs=pl.BlockSpec((tm, D_MODEL), row),
        out_shape=jax.ShapeDtypeStruct((n, D_MODEL), F32),
        scratch_shapes=[pltpu.VMEM((POOL_CARRY, POOL_WIDTH), F32)],
        compiler_params=pltpu.CompilerParams(
            dimension_semantics=("arbitrary", "arbitrary"), vmem_limit_bytes=WIDE_VMEM_LIMIT),
        name="merge",
    )(x, mod, yg, proj, pool0, proj, w_o, w_pool, pool_scale, w_out)


def _to_pairs(s):
    b = s.shape[0]
    s = s.reshape(b, N_PAIRS, 2, HEAD_SIZE, HEAD_SIZE)
    z = jnp.zeros_like(s[:, :, 0])
    top = jnp.concatenate([s[:, :, 0], z], axis=-1)
    bot = jnp.concatenate([z, s[:, :, 1]], axis=-1)
    return jnp.concatenate([top, bot], axis=-2)


def _from_pairs(sp):
    b = sp.shape[0]
    h0 = sp[:, :, :HEAD_SIZE, :HEAD_SIZE]
    h1 = sp[:, :, HEAD_SIZE:, HEAD_SIZE:]
    return jnp.stack([h0, h1], axis=2).reshape(b, N_HEADS, HEAD_SIZE, HEAD_SIZE)


def _stream_set(x, t, nb, b_off, *, ffn_tm, proj_tm, merge_tm, chunk, pos0):
    return dict(x=x.reshape(nb * t, D_MODEL), t=t, nb=nb, b_off=b_off, ffn_tm=ffn_tm, proj_tm=proj_tm,
                merge_tm=merge_tm, chunk=chunk, pos0=pos0)


def kernel(x_prompt, x_sample, c_prompt, c_sample, state_wkv, state_shift, state_pool, norm_g, w_mod, b_mod,
           ffn_w1, ffn_w3, ffn_w2, w_in, shift_mu, w0, w_decay, a0, w_a, w_gate, k_k, k_a, r_k, lnx_w, lnx_b,
           w_o_rwkv, w_pool, pool_scale, w_out, final_g):
    bp, seq, _ = x_prompt.shape
    bs, dseq, _ = x_sample.shape
    past_len = 1024
    dt = x_prompt.dtype

    w1 = ffn_w1.astype(BF16)
    w3 = ffn_w3.astype(BF16)
    w2 = ffn_w2.astype(BF16)
    pad = jnp.zeros((DEPTH, D_MODEL, GATE_COL - SHIFT_WIDTH), BF16)
    w_in_b = w_in.astype(BF16)
    w_in_p = jnp.concatenate([w_in_b[:, :, :SHIFT_WIDTH], pad, w_in_b[:, :, SHIFT_WIDTH + POOL_WIDTH:],
                              w_in_b[:, :, SHIFT_WIDTH:SHIFT_WIDTH + POOL_WIDTH]], axis=2)
    w_o_b = w_o_rwkv.astype(BF16)
    w_pool_b = w_pool.astype(BF16)
    w_out_b = w_out.astype(BF16)
    zr = lambda rows: jnp.zeros((DEPTH, rows, RWKV_WIDTH), F32)
    w_lr = jnp.concatenate([
        jnp.concatenate([w_decay, zr(LOWRANK_PAD - DECAY_RANK)], axis=1),
        jnp.concatenate([zr(DECAY_RANK), w_a, zr(LOWRANK_PAD - DECAY_RANK - A_RANK)], axis=1),
        jnp.concatenate([zr(DECAY_RANK + A_RANK), w_gate, zr(LOWRANK_PAD - LOWRANK)], axis=1),
    ], axis=2).astype(BF16)
    vecs = jnp.stack([w0, a0, k_k, k_a, lnx_w, lnx_b, r_k.reshape(DEPTH, RWKV_WIDTH),
                      jnp.zeros_like(w0)], axis=1)
    mu = jnp.pad(shift_mu, ((0, 0), (0, RWKV_COLS - SHIFT_WIDTH))).reshape(DEPTH, 1, RWKV_COLS)
    norm_g4 = norm_g.reshape(DEPTH, 3, 1, D_MODEL)
    fg = final_g.reshape(1, D_MODEL)
    ps = pool_scale.reshape(DEPTH, 1, D_MODEL)

    mod = _modulation(jnp.concatenate([c_prompt, c_sample], axis=0), w_mod, b_mod)
    mod = mod.reshape(DEPTH, bp + bs, N_MOD, D_MODEL)

    prompt = _stream_set(x_prompt, seq, bp, 0, ffn_tm=min(1024, seq), proj_tm=min(1024, seq),
                         merge_tm=min(512, seq), chunk=min(64, seq), pos0=0)
    sample = _stream_set(x_sample, dseq, bs, bp, ffn_tm=bs * dseq, proj_tm=bs * dseq, merge_tm=dseq,
                         chunk=dseq, pos0=past_len)

    outs = {"p": ([], [], []), "s": ([], [], [])}
    for l in range(DEPTH):
        for tag, st in (("p", prompt), ("s", sample)):
            nb, t = st["nb"], st["t"]
            if tag == "p":
                shift0 = jnp.zeros((nb, 1, RWKV_COLS), F32)
                s0 = jnp.zeros((nb, N_PAIRS, LANES, LANES), F32)
                pool0 = jnp.zeros((nb, POOL_CARRY, POOL_WIDTH), F32)
            else:
                shift0 = jnp.pad(state_shift[l], ((0, 0), (0, RWKV_COLS - SHIFT_WIDTH))).reshape(nb, 1, RWKV_COLS)
                s0 = _to_pairs(state_wkv[l].astype(F32))
                pool0 = jnp.pad(state_pool[l], ((0, 0), (POOL_CARRY - POOL_BUF, 0), (0, 0)))
            common = dict(layer=l, t=t, b_off=st["b_off"])
            x = st["x"]
            x = _ffn(x, mod, norm_g4, w1, w3, w2, fg, which=0, sub=0, tm=st["ffn_tm"], final=False, **common)
            proj = _in_proj(x, mod, norm_g4, w_in_p, tm=st["proj_tm"], **common)
            yg, s_new = _wkv(proj, shift0, s0, mu, vecs, w_lr, layer=l, nb=nb, t=t, chunk=st["chunk"])
            x = _merge(x, mod, yg, proj, pool0, w_o_b, w_pool_b, ps, w_out_b, nb=nb,
                       tm=st["merge_tm"], pos0=st["pos0"], **common)
            x = _ffn(x, mod, norm_g4, w1, w3, w2, fg, which=1, sub=2, tm=st["ffn_tm"],
                     final=(l == DEPTH - 1), **common)
            st["x"] = x
            wkv_l, shift_l, pool_l = outs[tag]
            wkv_l.append(_from_pairs(s_new).astype(dt))
            proj3 = proj.reshape(nb, t, PROJ_WIDTH)
            shift_l.append(proj3[:, -1, :SHIFT_WIDTH])
            pool_l.append(proj3[:, t - POOL_BUF:, POOL_COL:])

    yp = prompt["x"].reshape(bp, seq, D_MODEL)
    ys = sample["x"].reshape(bs, dseq, D_MODEL)
    pw, psh, ppl = (jnp.stack(v) for v in outs["p"])
    sw, ssh, spl = (jnp.stack(v) for v in outs["s"])
    return (yp, ys, pw, psh, ppl, sw, ssh, spl)
```

```python
import functools

import jax
import jax.numpy as jnp
from jax import lax
from jax.experimental import pallas as pl
from jax.experimental.pallas import tpu as pltpu

F32 = jnp.float32
BF16 = jnp.bfloat16

D_MODEL = 2048
DEPTH = 4
RWKV_WIDTH = D_MODEL // 2
HEAD_SIZE = 64
N_HEADS = RWKV_WIDTH // HEAD_SIZE
DECAY_RANK = 64
A_RANK = 64
GATE_RANK = 160
POOL_WIDTH = D_MODEL // 2
POOL_WINDOWS = (2, 4, 8, 16)
POOL_GROUP = POOL_WIDTH // len(POOL_WINDOWS)
POOL_OUT_GROUP = D_MODEL // len(POOL_WINDOWS)
POOL_BUF = max(POOL_WINDOWS) - 1
D_FF = ((8 * D_MODEL // 3 + 255) // 256) * 256
SHIFT_WIDTH = 3 * RWKV_WIDTH + DECAY_RANK + A_RANK + GATE_RANK
N_MOD = 9
RMS_EPS = 1e-6
GN_EPS = 64e-5

LANES = 128
N_PAIRS = RWKV_WIDTH // LANES
LOWRANK = DECAY_RANK + A_RANK + GATE_RANK
LOWRANK_PAD = 3 * LANES
RWKV_COLS = 3 * RWKV_WIDTH + LOWRANK_PAD
PROJ_TILE = 2304
GATE_COL = 2 * D_MODEL
POOL_COL = 4 * D_MODEL
PROJ_WIDTH = POOL_COL + POOL_WIDTH
POOL_CARRY = 16
VMEM_LIMIT = 56 * 1024 * 1024
WIDE_VMEM_LIMIT = 62 * 1024 * 1024


def _dot(a, b):
    return jnp.dot(a, b, preferred_element_type=F32)


def _dot_nt(a, b):
    return lax.dot_general(a, b, (((1,), (1,)), ((), ())), preferred_element_type=F32)


def _dot_tn(a, b):
    return lax.dot_general(a, b, (((0,), (0,)), ((), ())), preferred_element_type=F32)


def _rows(v, tm):
    g, d = v.shape
    if g == 1:
        return v
    return jnp.broadcast_to(v[:, None, :], (g, tm // g, d)).reshape(tm, d)


NORM_ROWS = 16


ROW_SPLIT = 4


def _modulated_norm_blocks(h_ref, x_ref, g_ref, m_ref, sub):
    tm = x_ref.shape[0]
    rows_per_stream = tm // m_ref.shape[0]
    block = tm // ROW_SPLIT
    gain = g_ref[...]
    cache = {}

    def scale_shift(s):
        if s not in cache:
            cache[s] = (gain * (1.0 + m_ref[s, 3 * sub + 1:3 * sub + 2, :]), m_ref[s, 3 * sub:3 * sub + 1, :])
        return cache[s]

    def store(r):
        for r0 in range(r * block, (r + 1) * block, NORM_ROWS):
            gs, shift = scale_shift(r0 // rows_per_stream)
            x = x_ref[r0:r0 + NORM_ROWS, :]
            y = x * lax.rsqrt(jnp.mean(x * x, axis=-1, keepdims=True) + RMS_EPS)
            h_ref[r0:r0 + NORM_ROWS, :] = (y * gs + shift).astype(BF16)

    return store, block


def _mod_kernel(c_ref, w_ref, b_ref, o_ref):
    c = c_ref[...]
    s = (c * jax.nn.sigmoid(c)).astype(BF16)
    o_ref[0] = _dot(s, w_ref[0].astype(BF16)) + b_ref[0]


def _modulation(c_all, w_mod, b_mod):
    nb = c_all.shape[0]
    width = N_MOD * D_MODEL
    tn = 1024
    return pl.pallas_call(
        _mod_kernel,
        grid=(DEPTH, width // tn),
        in_specs=[
            pl.BlockSpec((nb, D_MODEL), lambda l, j: (0, 0)),
            pl.BlockSpec((1, D_MODEL, tn), lambda l, j: (l, 0, j)),
            pl.BlockSpec((1, 1, tn), lambda l, j: (l, 0, j)),
        ],
        out_specs=pl.BlockSpec((1, nb, tn), lambda l, j: (l, 0, j)),
        out_shape=jax.ShapeDtypeStruct((DEPTH, nb, width), F32),
        compiler_params=pltpu.CompilerParams(
            dimension_semantics=("arbitrary", "arbitrary"), vmem_limit_bytes=VMEM_LIMIT),
        name="modulation",
    )(c_all, w_mod, b_mod.reshape(DEPTH, 1, width))


def _mod_spec(layer, tm, t, b_off):
    g = max(1, tm // t)
    tiles_per_stream = max(1, t // tm)
    assert b_off % g == 0 and (tm % t == 0 or t % tm == 0)
    base = b_off // g
    return pl.BlockSpec((None, g, N_MOD, D_MODEL),
                        lambda i, j: (layer, base + i // tiles_per_stream, 0, 0))


FF_TILE = 512


def _ffn_kernel(x_ref, m_ref, g_ref, w1_ref, w3_ref, w2_ref, fg_ref, o_ref, h_ref, *, sub, final):
    j = pl.program_id(1)
    last = pl.num_programs(1) - 1
    tm = x_ref.shape[0]

    def gated(h):
        a = _dot(h, w1_ref[...])
        return (a * jax.nn.sigmoid(a) * _dot(h, w3_ref[...])).astype(BF16)

    @pl.when(j == 0)
    def _():
        store_norm, block = _modulated_norm_blocks(h_ref, x_ref, g_ref, m_ref, sub)
        store_norm(0)
        for r in range(ROW_SPLIT):
            if r + 1 < ROW_SPLIT:
                store_norm(r + 1)
            rows = slice(r * block, (r + 1) * block)
            o_ref[rows, :] = _dot(gated(h_ref[rows, :]), w2_ref[...])

    @pl.when(jnp.logical_and(j > 0, j < last))
    def _():
        o_ref[...] += _dot(gated(h_ref[...]), w2_ref[...])

    @pl.when(j == last)
    def _():
        block = tm // ROW_SPLIT
        u = gated(h_ref[...])
        gate = 0.5 * _rows(m_ref[...][:, 3 * sub + 2, :], tm)
        for r in range(ROW_SPLIT):
            rows = slice(r * block, (r + 1) * block)
            acc = o_ref[rows, :] + _dot(u[rows, :], w2_ref[...])
            y = x_ref[rows, :] + (gate if gate.shape[0] == 1 else gate[rows, :]) * acc
            if final:
                y = y * lax.rsqrt(jnp.mean(y * y, axis=-1, keepdims=True) + RMS_EPS) * fg_ref[...]
            o_ref[rows, :] = y


def _ffn(x, mod, norm_g, w1, w3, w2, final_g, *, layer, which, sub, t, b_off, tm, final):
    n = x.shape[0]
    tf = FF_TILE
    assert D_FF // tf >= 2
    kern = functools.partial(_ffn_kernel, sub=sub, final=final)
    return pl.pallas_call(
        kern,
        grid=(n // tm, D_FF // tf),
        in_specs=[
            pl.BlockSpec((tm, D_MODEL), lambda i, j: (i, 0)),
            _mod_spec(layer, tm, t, b_off),
            pl.BlockSpec((None, None, 1, D_MODEL), lambda i, j: (layer, sub, 0, 0)),
            pl.BlockSpec((None, None, D_MODEL, tf), lambda i, j: (layer, which, 0, j)),
            pl.BlockSpec((None, None, D_MODEL, tf), lambda i, j: (layer, which, 0, j)),
            pl.BlockSpec((None, None, tf, D_MODEL), lambda i, j: (layer, which, j, 0)),
            pl.BlockSpec((1, D_MODEL), lambda i, j: (0, 0)),
        ],
        out_specs=pl.BlockSpec((tm, D_MODEL), lambda i, j: (i, 0)),
        out_shape=jax.ShapeDtypeStruct((n, D_MODEL), F32),
        scratch_shapes=[pltpu.VMEM((tm, D_MODEL), BF16)],
        compiler_params=pltpu.CompilerParams(
            dimension_semantics=("arbitrary", "arbitrary"), vmem_limit_bytes=WIDE_VMEM_LIMIT),
        name="ffn",
    )(x, mod, norm_g, w1, w3, w2, final_g)


def _proj_kernel(x_ref, m_ref, g_ref, w_ref, o_ref, h_ref):
    j = pl.program_id(1)

    @pl.when(j == 0)
    def _():
        store_norm, block = _modulated_norm_blocks(h_ref, x_ref, g_ref, m_ref, 1)
        store_norm(0)
        for r in range(ROW_SPLIT):
            if r + 1 < ROW_SPLIT:
                store_norm(r + 1)
            rows = slice(r * block, (r + 1) * block)
            o_ref[rows, :] = _dot(h_ref[rows, :], w_ref[...])

    @pl.when(j > 0)
    def _():
        o_ref[...] = _dot(h_ref[...], w_ref[...])


def _in_proj(x, mod, norm_g, w_in, *, layer, t, b_off, tm):
    n = x.shape[0]
    return pl.pallas_call(
        _proj_kernel,
        grid=(n // tm, PROJ_WIDTH // PROJ_TILE),
        in_specs=[
            pl.BlockSpec((tm, D_MODEL), lambda i, j: (i, 0)),
            _mod_spec(layer, tm, t, b_off),
            pl.BlockSpec((None, None, 1, D_MODEL), lambda i, j: (layer, 1, 0, 0)),
            pl.BlockSpec((None, D_MODEL, PROJ_TILE), lambda i, j: (layer, 0, j)),
        ],
        out_specs=pl.BlockSpec((tm, PROJ_TILE), lambda i, j: (i, j)),
        out_shape=jax.ShapeDtypeStruct((n, PROJ_WIDTH), F32),
        scratch_shapes=[pltpu.VMEM((tm, D_MODEL), BF16)],
        compiler_params=pltpu.CompilerParams(
            dimension_semantics=("arbitrary", "arbitrary"), vmem_limit_bytes=WIDE_VMEM_LIMIT),
        name="in_proj",
    )(x, mod, norm_g, w_in)


def _split3(x):
    hi = x.astype(BF16)
    r1 = x - hi.astype(F32)
    mid = r1.astype(BF16)
    lo = (r1 - mid.astype(F32)).astype(BF16)
    return hi, mid, lo


def _wkv_kernel(p_ref, sh0_ref, s0_ref, mu_ref, vec_ref, wlr_ref, o_ref, sout_ref, state_ref, prev_ref, *, chunk):
    c = pl.program_id(1)
    C = chunk
    C2 = 2 * C
    n_streams = p_ref.shape[1]

    @pl.when(c == 0)
    def _():
        state_ref[...] = s0_ref[0]
        prev_ref[...] = sh0_ref[0]

    w0, a0, k_k, k_a, lnx_w, lnx_b, r_k = (vec_ref[i:i + 1, :] for i in range(7))
    lane = lax.broadcasted_iota(jnp.int32, (1, LANES), 1)
    head0 = lane < HEAD_SIZE
    ri = lax.broadcasted_iota(jnp.int32, (LANES, LANES), 0)
    ci = lax.broadcasted_iota(jnp.int32, (LANES, LANES), 1)
    head_ones = ((ri < HEAD_SIZE) == (ci < HEAD_SIZE)).astype(BF16)
    tr = lax.broadcasted_iota(jnp.int32, (C, C), 0)
    tc = lax.broadcasted_iota(jnp.int32, (C, C), 1)
    cum_ones = (tc <= tr).astype(BF16)
    row = lax.broadcasted_iota(jnp.int32, (C, 1), 0)

    def head_sum(x):
        st = jnp.concatenate([x[:, g * LANES:(g + 1) * LANES] for g in range(N_PAIRS)], axis=0)
        s = _dot(st.astype(BF16), head_ones)
        return jnp.concatenate([s[g * C:(g + 1) * C] for g in range(N_PAIRS)], axis=1)

    def prepare(s):
        p = p_ref[0, s]
        prev = jnp.where(row == 0, prev_ref[s], pltpu.roll(p, 1, 0))
        prev_ref[s] = p[C - 1:C, :]
        xs = p + (prev - p) * mu_ref[...]

        r = xs[:, 0:RWKV_WIDTH]
        k = xs[:, RWKV_WIDTH:2 * RWKV_WIDTH]
        v = xs[:, 2 * RWKV_WIDTH:3 * RWKV_WIDTH]
        z = xs[:, 3 * RWKV_WIDTH:RWKV_COLS]
        zc = lax.broadcasted_iota(jnp.int32, z.shape, 1)
        zact = jnp.where(zc < DECAY_RANK, jnp.tanh(z),
                         jnp.where(zc < DECAY_RANK + A_RANK, z, jax.nn.sigmoid(z)))
        lr = _dot(zact.astype(BF16), wlr_ref[...])

        zz = -(w0 + lr[:, 0:RWKV_WIDTH])
        softplus = jnp.maximum(zz, 0.0) + jnp.log(1.0 + jnp.exp(-jnp.abs(zz)))
        lw = -jnp.exp(-softplus - 0.5)
        a = jax.nn.sigmoid(a0 + lr[:, RWKV_WIDTH:2 * RWKV_WIDTH])

        hi, mid, lo = _split3(lw)
        cs = _dot(cum_ones, hi) + _dot(cum_ones, mid) + _dot(cum_ones, lo)
        w_in = jnp.exp(cs)
        w_inv = jnp.exp(-cs)
        w_ex = jnp.exp(cs - lw)

        kk = k * k_k
        k2 = k * (1.0 + (a - 1.0) * k_a)
        kk = kk * lax.rsqrt(jnp.maximum(head_sum(kk * kk), 1e-24))
        b = kk * a
        k_h = k2 * w_inv
        b_h = b * w_inv
        w_c = w_in[C - 1:C, :]
        return dict(v=v, r_t=r * w_in, kk_t=kk * w_ex, k_h=k_h, b_h=b_h, w_c=w_c, k_b=k_h * w_c, b_b=b_h * w_c,
                    gate=lr[:, 2 * RWKV_WIDTH:3 * RWKV_WIDTH], bonus=head_sum(r * k2 * r_k) * v)

    pre = [prepare(s) for s in range(n_streams)]

    tr2 = lax.broadcasted_iota(jnp.int32, (C, C2), 0)
    tc2 = lax.broadcasted_iota(jnp.int32, (C, C2), 1)
    col_head0 = tc2 < C
    ts2 = jnp.where(col_head0, tc2, tc2 - C)
    strict = ts2 < tr2
    incl = ts2 <= tr2

    def stack(x):
        xb = x.astype(BF16)
        zero = jnp.zeros_like(xb)
        return jnp.concatenate([jnp.where(head0, xb, zero), jnp.where(head0, zero, xb)], axis=0)

    def block_diag(p):
        pb = p.astype(BF16)
        zero = jnp.zeros_like(pb)
        return jnp.concatenate([jnp.where(col_head0, pb, zero), jnp.where(col_head0, zero, pb)], axis=0)

    chains = [(s, g) for s in range(n_streams) for g in range(N_PAIRS)]
    pairs = range(len(chains))

    def lanes(name, i):
        s, g = chains[i]
        return pre[s][name][:, g * LANES:(g + 1) * LANES]

    s_prev = [state_ref[s, g] for s, g in chains]
    q = [jnp.concatenate([lanes("kk_t", i), lanes("r_t", i)], axis=0).astype(BF16) for i in pairs]
    kb = [jnp.concatenate([stack(lanes("k_h", i)), stack(lanes("b_h", i))], axis=0) for i in pairs]
    s_v = [stack(lanes("v", i)) for i in pairs]
    kb_c = [jnp.concatenate([stack(lanes("k_b", i)), -stack(lanes("b_b", i))], axis=0) for i in pairs]

    scores = [_dot_nt(q[g], kb[g]) for g in pairs]
    proj = [_dot_nt(q[g], s_prev[g].astype(BF16)) for g in pairs]
    a_kk_k = [jnp.where(strict, s[:C, :C2], 0.0).astype(BF16) for s in scores]
    m_b = [jnp.where(strict, s[:C, C2:], 0.0) for s in scores]
    a_r = [jnp.concatenate([jnp.where(incl, s[C:, :C2], 0.0), jnp.where(incl, -s[C:, C2:], 0.0)],
                           axis=1).astype(BF16) for s in scores]
    z = [_dot(a_kk_k[g], s_v[g]) + proj[g][:C] for g in pairs]

    pw = m_b
    sign = -1.0
    n = 1
    while n < C:
        last = 2 * n >= C
        nxt = []
        for g in pairs:
            if last:
                pz = _dot(pw[g].astype(BF16), stack(z[g]))
            else:
                both = _dot(pw[g].astype(BF16), jnp.concatenate([block_diag(pw[g]), stack(z[g])], axis=1))
                nxt.append(both[:, :C2])
                pz = both[:, C2:]
            z[g] = z[g] + sign * pz
        pw = nxt
        sign = 1.0
        n *= 2

    v_sa = [jnp.concatenate([s_v[g], stack(z[g])], axis=0) for g in pairs]
    ys = []
    for i in pairs:
        s, g = chains[i]
        ys.append(proj[i][C:] + _dot(a_r[i], v_sa[i]))
        state_ref[s, g] = s_prev[i] * lanes("w_c", i) + _dot_tn(v_sa[i], kb_c[i])

    for s in range(n_streams):
        y = jnp.concatenate(ys[s * N_PAIRS:(s + 1) * N_PAIRS], axis=1)
        mean = head_sum(y) * (1.0 / HEAD_SIZE)
        d = y - mean
        var = head_sum(d * d) * (1.0 / HEAD_SIZE)
        yn = d * lax.rsqrt(var + GN_EPS) * lnx_w + lnx_b + pre[s]["bonus"]
        o_ref[0, s] = (yn * pre[s]["gate"]).astype(BF16)

    @pl.when(c == pl.num_programs(1) - 1)
    def _():
        sout_ref[0] = state_ref[...]


WKV_STREAMS = 4


def _wkv(proj, shift0, s0_pairs, mu, vecs, w_lowrank, *, layer, nb, t, chunk):
    ns = WKV_STREAMS
    nc = t // chunk
    kern = functools.partial(_wkv_kernel, chunk=chunk)
    grouped = lambda a: a.reshape((nb // ns, ns) + a.shape[1:])
    yg, s_new = pl.pallas_call(
        kern,
        grid=(nb // ns, nc),
        in_specs=[
            pl.BlockSpec((1, ns, chunk, RWKV_COLS), lambda b, c: (b, 0, c, 0)),
            pl.BlockSpec((1, ns, 1, RWKV_COLS), lambda b, c: (b, 0, 0, 0)),
            pl.BlockSpec((1, ns, N_PAIRS, LANES, LANES), lambda b, c: (b, 0, 0, 0, 0)),
            pl.BlockSpec((None, 1, RWKV_COLS), lambda b, c: (layer, 0, 0)),
            pl.BlockSpec((None, 8, RWKV_WIDTH), lambda b, c: (layer, 0, 0)),
            pl.BlockSpec((None, LOWRANK_PAD, 3 * RWKV_WIDTH), lambda b, c: (layer, 0, 0)),
        ],
        out_specs=[
            pl.BlockSpec((1, ns, chunk, RWKV_WIDTH), lambda b, c: (b, 0, c, 0)),
            pl.BlockSpec((1, ns, N_PAIRS, LANES, LANES), lambda b, c: (b, 0, 0, 0, 0)),
        ],
        out_shape=[
            jax.ShapeDtypeStruct((nb // ns, ns, t, RWKV_WIDTH), BF16),
            jax.ShapeDtypeStruct((nb // ns, ns, N_PAIRS, LANES, LANES), F32),
        ],
        scratch_shapes=[pltpu.VMEM((ns, N_PAIRS, LANES, LANES), F32), pltpu.VMEM((ns, 1, RWKV_COLS), F32)],
        compiler_params=pltpu.CompilerParams(
            dimension_semantics=("arbitrary", "arbitrary"), vmem_limit_bytes=VMEM_LIMIT),
        name="wkv",
    )(proj.reshape(nb // ns, ns, t, PROJ_WIDTH), grouped(shift0), grouped(s0_pairs), mu, vecs, w_lowrank)
    return yg.reshape(nb * t, RWKV_WIDTH), s_new.reshape(nb, N_PAIRS, LANES, LANES)


def _merge_kernel(x_ref, m_ref, yg_ref, pp_ref, buf_ref, gt_ref, wo_ref, wp_ref, ps_ref, wout_ref, o_ref, carry_ref,
                  *, pos0, tm):
    i = pl.program_id(1)

    @pl.when(i == 0)
    def _():
        carry_ref[...] = buf_ref[0]

    xp = pp_ref[...]
    full = jnp.concatenate([carry_ref[...], xp], axis=0)
    carry_ref[...] = full[tm:tm + POOL_CARRY]

    pos = pos0 + i * tm + lax.broadcasted_iota(jnp.int32, (tm, POOL_GROUP), 0)
    o_b = []
    for gi, win in enumerate(POOL_WINDOWS):
        cols = slice(gi * POOL_GROUP, (gi + 1) * POOL_GROUP)
        s = full[:, cols]
        width = 1
        while width < win:
            s = s[width:] + s[:-width]
            width *= 2
        s = s[POOL_CARRY - (win - 1):]
        cnt = jnp.minimum(win, pos + 1).astype(F32)
        pooled = s / cnt - xp[:, cols]
        o_b.append(_dot(pooled.astype(BF16), wp_ref[gi]))
    o_b = jnp.concatenate(o_b, axis=1) * ps_ref[...]
    o_a = _dot(yg_ref[...], wo_ref[...])
    gates = jax.nn.sigmoid(gt_ref[...])
    merged = gates[:, :D_MODEL] * o_a + gates[:, D_MODEL:] * o_b
    o_ref[...] = x_ref[...] + m_ref[...][:, 5, :] * _dot(merged.astype(BF16), wout_ref[...])


def _merge(x, mod, yg, proj, pool0, w_o, w_pool, pool_scale, w_out, *, layer, nb, t, b_off, tm, pos0):
    n = x.shape[0]
    nt = t // tm
    kern = functools.partial(_merge_kernel, pos0=pos0, tm=tm)
    row = lambda b, i: (b * nt + i, 0)
    return pl.pallas_call(
        kern,
        grid=(nb, nt),
        in_specs=[
            pl.BlockSpec((tm, D_MODEL), row),
            pl.BlockSpec((None, 1, N_MOD, D_MODEL), lambda b, i: (layer, b_off + b, 0, 0)),
            pl.BlockSpec((tm, RWKV_WIDTH), row),
            pl.BlockSpec((tm, POOL_WIDTH), lambda b, i: (b * nt + i, POOL_COL // POOL_WIDTH)),
            pl.BlockSpec((1, POOL_CARRY, POOL_WIDTH), lambda b, i: (b, 0, 0)),
            pl.BlockSpec((tm, 2 * D_MODEL), lambda b, i: (b * nt + i, GATE_COL // (2 * D_MODEL))),
            pl.BlockSpec((None, RWKV_WIDTH, D_MODEL), lambda b, i: (layer, 0, 0), pipeline_mode=pl.Buffered(1)),
            pl.BlockSpec((None, len(POOL_WINDOWS), POOL_GROUP, POOL_OUT_GROUP), lambda b, i: (layer, 0, 0, 0),
                         pipeline_mode=pl.Buffered(1)),
            pl.BlockSpec((None, 1, D_MODEL), lambda b, i: (layer, 0, 0)),
            pl.BlockSpec((None, D_MODEL, D_MODEL), lambda b, i: (layer, 0, 0), pipeline_mode=pl.Buffered(1)),
        ],
        out_specs=pl.BlockSpec((tm, D_MODEL), row),
        out_shape=jax.ShapeDtypeStruct((n, D_MODEL), F32),
        scratch_shapes=[pltpu.VMEM((POOL_CARRY, POOL_WIDTH), F32)],
        compiler_params=pltpu.CompilerParams(
            dimension_semantics=("arbitrary", "arbitrary"), vmem_limit_bytes=WIDE_VMEM_LIMIT),
        name="merge",
    )(x, mod, yg, proj, pool0, proj, w_o, w_pool, pool_scale, w_out)


def _to_pairs(s):
    b = s.shape[0]
    s = s.reshape(b, N_PAIRS, 2, HEAD_SIZE, HEAD_SIZE)
    z = jnp.zeros_like(s[:, :, 0])
    top = jnp.concatenate([s[:, :, 0], z], axis=-1)
    bot = jnp.concatenate([z, s[:, :, 1]], axis=-1)
    return jnp.concatenate([top, bot], axis=-2)


def _from_pairs(sp):
    b = sp.shape[0]
    h0 = sp[:, :, :HEAD_SIZE, :HEAD_SIZE]
    h1 = sp[:, :, HEAD_SIZE:, HEAD_SIZE:]
    return jnp.stack([h0, h1], axis=2).reshape(b, N_HEADS, HEAD_SIZE, HEAD_SIZE)


def _stream_set(x, t, nb, b_off, *, ffn_tm, proj_tm, merge_tm, chunk, pos0):
    return dict(x=x.reshape(nb * t, D_MODEL), t=t, nb=nb, b_off=b_off, ffn_tm=ffn_tm, proj_tm=proj_tm,
                merge_tm=merge_tm, chunk=chunk, pos0=pos0)


def kernel(x_prompt, x_sample, c_prompt, c_sample, state_wkv, state_shift, state_pool, norm_g, w_mod, b_mod,
           ffn_w1, ffn_w3, ffn_w2, w_in, shift_mu, w0, w_decay, a0, w_a, w_gate, k_k, k_a, r_k, lnx_w, lnx_b,
           w_o_rwkv, w_pool, pool_scale, w_out, final_g):
    bp, seq, _ = x_prompt.shape
    bs, dseq, _ = x_sample.shape
    past_len = 1024
    dt = x_prompt.dtype

    w1 = ffn_w1.astype(BF16)
    w3 = ffn_w3.astype(BF16)
    w2 = ffn_w2.astype(BF16)
    pad = jnp.zeros((DEPTH, D_MODEL, GATE_COL - SHIFT_WIDTH), BF16)
    w_in_b = w_in.astype(BF16)
    w_in_p = jnp.concatenate([w_in_b[:, :, :SHIFT_WIDTH], pad, w_in_b[:, :, SHIFT_WIDTH + POOL_WIDTH:],
                              w_in_b[:, :, SHIFT_WIDTH:SHIFT_WIDTH + POOL_WIDTH]], axis=2)
    w_o_b = w_o_rwkv.astype(BF16)
    w_pool_b = w_pool.astype(BF16)
    w_out_b = w_out.astype(BF16)
    zr = lambda rows: jnp.zeros((DEPTH, rows, RWKV_WIDTH), F32)
    w_lr = jnp.concatenate([
        jnp.concatenate([w_decay, zr(LOWRANK_PAD - DECAY_RANK)], axis=1),
        jnp.concatenate([zr(DECAY_RANK), w_a, zr(LOWRANK_PAD - DECAY_RANK - A_RANK)], axis=1),
        jnp.concatenate([zr(DECAY_RANK + A_RANK), w_gate, zr(LOWRANK_PAD - LOWRANK)], axis=1),
    ], axis=2).astype(BF16)
    vecs = jnp.stack([w0, a0, k_k, k_a, lnx_w, lnx_b, r_k.reshape(DEPTH, RWKV_WIDTH),
                      jnp.zeros_like(w0)], axis=1)
    mu = jnp.pad(shift_mu, ((0, 0), (0, RWKV_COLS - SHIFT_WIDTH))).reshape(DEPTH, 1, RWKV_COLS)
    norm_g4 = norm_g.reshape(DEPTH, 3, 1, D_MODEL)
    fg = final_g.reshape(1, D_MODEL)
    ps = pool_scale.reshape(DEPTH, 1, D_MODEL)

    mod = _modulation(jnp.concatenate([c_prompt, c_sample], axis=0), w_mod, b_mod)
    mod = mod.reshape(DEPTH, bp + bs, N_MOD, D_MODEL)

    prompt = _stream_set(x_prompt, seq, bp, 0, ffn_tm=min(1024, seq), proj_tm=min(1024, seq),
                         merge_tm=min(512, seq), chunk=min(64, seq), pos0=0)
    sample = _stream_set(x_sample, dseq, bs, bp, ffn_tm=bs * dseq, proj_tm=bs * dseq, merge_tm=dseq,
                         chunk=dseq, pos0=past_len)

    outs = {"p": ([], [], []), "s": ([], [], [])}
    for l in range(DEPTH):
        for tag, st in (("p", prompt), ("s", sample)):
            nb, t = st["nb"], st["t"]
            if tag == "p":
                shift0 = jnp.zeros((nb, 1, RWKV_COLS), F32)
                s0 = jnp.zeros((nb, N_PAIRS, LANES, LANES), F32)
                pool0 = jnp.zeros((nb, POOL_CARRY, POOL_WIDTH), F32)
            else:
                shift0 = jnp.pad(state_shift[l], ((0, 0), (0, RWKV_COLS - SHIFT_WIDTH))).reshape(nb, 1, RWKV_COLS)
                s0 = _to_pairs(state_wkv[l].astype(F32))
                pool0 = jnp.pad(state_pool[l], ((0, 0), (POOL_CARRY - POOL_BUF, 0), (0, 0)))
            common = dict(layer=l, t=t, b_off=st["b_off"])
            x = st["x"]
            x = _ffn(x, mod, norm_g4, w1, w3, w2, fg, which=0, sub=0, tm=st["ffn_tm"], final=False, **common)
            proj = _in_proj(x, mod, norm_g4, w_in_p, tm=st["proj_tm"], **common)
            yg, s_new = _wkv(proj, shift0, s0, mu, vecs, w_lr, layer=l, nb=nb, t=t, chunk=st["chunk"])
            x = _merge(x, mod, yg, proj, pool0, w_o_b, w_pool_b, ps, w_out_b, nb=nb,
                       tm=st["merge_tm"], pos0=st["pos0"], **common)
            x = _ffn(x, mod, norm_g4, w1, w3, w2, fg, which=1, sub=2, tm=st["ffn_tm"],
                     final=(l == DEPTH - 1), **common)
            st["x"] = x
            wkv_l, shift_l, pool_l = outs[tag]
            wkv_l.append(_from_pairs(s_new).astype(dt))
            proj3 = proj.reshape(nb, t, PROJ_WIDTH)
            shift_l.append(proj3[:, -1, :SHIFT_WIDTH])
            pool_l.append(proj3[:, t - POOL_BUF:, POOL_COL:])

    yp = prompt["x"].reshape(bp, seq, D_MODEL)
    ys = sample["x"].reshape(bs, dseq, D_MODEL)
    pw, psh, ppl = (jnp.stack(v) for v in outs["p"])
    sw, ssh, spl = (jnp.stack(v) for v in outs["s"])
    return (yp, ys, pw, psh, ppl, sw, ssh, spl)
```

```python
import functools

import jax
import jax.numpy as jnp
from jax import lax
from jax.experimental import pallas as pl
from jax.experimental.pallas import tpu as pltpu

F32 = jnp.float32
BF16 = jnp.bfloat16

D_MODEL = 2048
DEPTH = 4
RWKV_WIDTH = D_MODEL // 2
HEAD_SIZE = 64
N_HEADS = RWKV_WIDTH // HEAD_SIZE
DECAY_RANK = 64
A_RANK = 64
GATE_RANK = 160
POOL_WIDTH = D_MODEL // 2
POOL_WINDOWS = (2, 4, 8, 16)
POOL_GROUP = POOL_WIDTH // len(POOL_WINDOWS)
POOL_OUT_GROUP = D_MODEL // len(POOL_WINDOWS)
POOL_BUF = max(POOL_WINDOWS) - 1
D_FF = ((8 * D_MODEL // 3 + 255) // 256) * 256
SHIFT_WIDTH = 3 * RWKV_WIDTH + DECAY_RANK + A_RANK + GATE_RANK
N_MOD = 9
RMS_EPS = 1e-6
GN_EPS = 64e-5

LANES = 128
N_PAIRS = RWKV_WIDTH // LANES
LOWRANK = DECAY_RANK + A_RANK + GATE_RANK
LOWRANK_PAD = 3 * LANES
RWKV_COLS = 3 * RWKV_WIDTH + LOWRANK_PAD
PROJ_TILE = 2304
GATE_COL = 2 * D_MODEL
POOL_COL = 4 * D_MODEL
PROJ_WIDTH = POOL_COL + POOL_WIDTH
POOL_CARRY = 16
VMEM_LIMIT = 56 * 1024 * 1024
WIDE_VMEM_LIMIT = 62 * 1024 * 1024


def _dot(a, b):
    return jnp.dot(a, b, preferred_element_type=F32)


def _dot_nt(a, b):
    return lax.dot_general(a, b, (((1,), (1,)), ((), ())), preferred_element_type=F32)


def _dot_tn(a, b):
    return lax.dot_general(a, b, (((0,), (0,)), ((), ())), preferred_element_type=F32)


def _rows(v, tm):
    g, d = v.shape
    if g == 1:
        return v
    return jnp.broadcast_to(v[:, None, :], (g, tm // g, d)).reshape(tm, d)


NORM_ROWS = 16


ROW_SPLIT = 4


def _modulated_norm_blocks(h_ref, x_ref, g_ref, m_ref, sub):
    tm = x_ref.shape[0]
    rows_per_stream = tm // m_ref.shape[0]
    block = tm // ROW_SPLIT
    gain = g_ref[...]
    cache = {}

    def scale_shift(s):
        if s not in cache:
            cache[s] = (gain * (1.0 + m_ref[s, 3 * sub + 1:3 * sub + 2, :]), m_ref[s, 3 * sub:3 * sub + 1, :])
        return cache[s]

    def store(r):
        for r0 in range(r * block, (r + 1) * block, NORM_ROWS):
            gs, shift = scale_shift(r0 // rows_per_stream)
            x = x_ref[r0:r0 + NORM_ROWS, :]
            y = x * lax.rsqrt(jnp.mean(x * x, axis=-1, keepdims=True) + RMS_EPS)
            h_ref[r0:r0 + NORM_ROWS, :] = (y * gs + shift).astype(BF16)

    return store, block


def _mod_kernel(c_ref, w_ref, b_ref, o_ref):
    c = c_ref[...]
    s = (c * jax.nn.sigmoid(c)).astype(BF16)
    o_ref[0] = _dot(s, w_ref[0].astype(BF16)) + b_ref[0]


def _modulation(c_all, w_mod, b_mod):
    nb = c_all.shape[0]
    width = N_MOD * D_MODEL
    tn = 1024
    return pl.pallas_call(
        _mod_kernel,
        grid=(DEPTH, width // tn),
        in_specs=[
            pl.BlockSpec((nb, D_MODEL), lambda l, j: (0, 0)),
            pl.BlockSpec((1, D_MODEL, tn), lambda l, j: (l, 0, j)),
            pl.BlockSpec((1, 1, tn), lambda l, j: (l, 0, j)),
        ],
        out_specs=pl.BlockSpec((1, nb, tn), lambda l, j: (l, 0, j)),
        out_shape=jax.ShapeDtypeStruct((DEPTH, nb, width), F32),
        compiler_params=pltpu.CompilerParams(
            dimension_semantics=("arbitrary", "arbitrary"), vmem_limit_bytes=VMEM_LIMIT),
        name="modulation",
    )(c_all, w_mod, b_mod.reshape(DEPTH, 1, width))


def _mod_spec(layer, tm, t, b_off):
    g = max(1, tm // t)
    tiles_per_stream = max(1, t // tm)
    assert b_off % g == 0 and (tm % t == 0 or t % tm == 0)
    base = b_off // g
    return pl.BlockSpec((None, g, N_MOD, D_MODEL),
                        lambda i, j: (layer, base + i // tiles_per_stream, 0, 0))


FF_TILE = 512


def _ffn_kernel(x_ref, m_ref, g_ref, w1_ref, w3_ref, w2_ref, fg_ref, o_ref, *rest, sub, final, emit_bf16):
    if emit_bf16:
        w1b_ref, w3b_ref, w2b_ref, h_ref = rest
        w1b_ref[...] = w1_ref[...].astype(BF16)
        w3b_ref[...] = w3_ref[...].astype(BF16)
        w2b_ref[...] = w2_ref[...].astype(BF16)
        w1_ref, w3_ref, w2_ref = w1b_ref, w3b_ref, w2b_ref
    else:
        (h_ref,) = rest
    j = pl.program_id(1)
    last = pl.num_programs(1) - 1
    tm = x_ref.shape[0]

    def gated(h):
        a = _dot(h, w1_ref[...])
        return (a * jax.nn.sigmoid(a) * _dot(h, w3_ref[...])).astype(BF16)

    @pl.when(j == 0)
    def _():
        store_norm, block = _modulated_norm_blocks(h_ref, x_ref, g_ref, m_ref, sub)
        store_norm(0)
        for r in range(ROW_SPLIT):
            if r + 1 < ROW_SPLIT:
                store_norm(r + 1)
            rows = slice(r * block, (r + 1) * block)
            o_ref[rows, :] = _dot(gated(h_ref[rows, :]), w2_ref[...])

    @pl.when(jnp.logical_and(j > 0, j < last))
    def _():
        o_ref[...] += _dot(gated(h_ref[...]), w2_ref[...])

    @pl.when(j == last)
    def _():
        block = tm // ROW_SPLIT
        u = gated(h_ref[...])
        gate = 0.5 * _rows(m_ref[...][:, 3 * sub + 2, :], tm)
        for r in range(ROW_SPLIT):
            rows = slice(r * block, (r + 1) * block)
            acc = o_ref[rows, :] + _dot(u[rows, :], w2_ref[...])
            y = x_ref[rows, :] + (gate if gate.shape[0] == 1 else gate[rows, :]) * acc
            if final:
                y = y * lax.rsqrt(jnp.mean(y * y, axis=-1, keepdims=True) + RMS_EPS) * fg_ref[...]
            o_ref[rows, :] = y


def _ffn(x, mod, norm_g, w1, w3, w2, final_g, *, layer, which, sub, t, b_off, tm, final, emit_bf16=False):
    n = x.shape[0]
    tf = FF_TILE
    assert D_FF // tf >= 2
    kern = functools.partial(_ffn_kernel, sub=sub, final=final, emit_bf16=emit_bf16)
    w13_tile = pl.BlockSpec((D_MODEL, tf), lambda i, j: (0, j))
    w2_tile = pl.BlockSpec((tf, D_MODEL), lambda i, j: (j, 0))
    out_specs = pl.BlockSpec((tm, D_MODEL), lambda i, j: (i, 0))
    out_shape = jax.ShapeDtypeStruct((n, D_MODEL), F32)
    if emit_bf16:
        assert n == tm
        w_specs = [
            pl.BlockSpec((None, None, D_MODEL, tf), lambda i, j: (layer, which, 0, j)),
            pl.BlockSpec((None, None, D_MODEL, tf), lambda i, j: (layer, which, 0, j)),
            pl.BlockSpec((None, None, tf, D_MODEL), lambda i, j: (layer, which, j, 0)),
        ]
        out_specs = [out_specs, w13_tile, w13_tile, w2_tile]
        out_shape = [out_shape, jax.ShapeDtypeStruct((D_MODEL, D_FF), BF16),
                     jax.ShapeDtypeStruct((D_MODEL, D_FF), BF16), jax.ShapeDtypeStruct((D_FF, D_MODEL), BF16)]
    else:
        w_specs = [w13_tile, w13_tile, w2_tile]
    return pl.pallas_call(
        kern,
        grid=(n // tm, D_FF // tf),
        in_specs=[
            pl.BlockSpec((tm, D_MODEL), lambda i, j: (i, 0)),
            _mod_spec(layer, tm, t, b_off),
            pl.BlockSpec((None, None, 1, D_MODEL), lambda i, j: (layer, sub, 0, 0)),
            *w_specs,
            pl.BlockSpec((1, D_MODEL), lambda i, j: (0, 0)),
        ],
        out_specs=out_specs,
        out_shape=out_shape,
        scratch_shapes=[pltpu.VMEM((tm, D_MODEL), BF16)],
        compiler_params=pltpu.CompilerParams(
            dimension_semantics=("arbitrary", "arbitrary"), vmem_limit_bytes=WIDE_VMEM_LIMIT),
        name="ffn",
    )(x, mod, norm_g, w1, w3, w2, final_g)


def _proj_kernel(x_ref, m_ref, g_ref, w_ref, o_ref, h_ref):
    j = pl.program_id(1)

    @pl.when(j == 0)
    def _():
        store_norm, block = _modulated_norm_blocks(h_ref, x_ref, g_ref, m_ref, 1)
        store_norm(0)
        for r in range(ROW_SPLIT):
            if r + 1 < ROW_SPLIT:
                store_norm(r + 1)
            rows = slice(r * block, (r + 1) * block)
            o_ref[rows, :] = _dot(h_ref[rows, :], w_ref[...])

    @pl.when(j > 0)
    def _():
        o_ref[...] = _dot(h_ref[...], w_ref[...])


def _in_proj(x, mod, norm_g, w_in, *, layer, t, b_off, tm):
    n = x.shape[0]
    return pl.pallas_call(
        _proj_kernel,
        grid=(n // tm, PROJ_WIDTH // PROJ_TILE),
        in_specs=[
            pl.BlockSpec((tm, D_MODEL), lambda i, j: (i, 0)),
            _mod_spec(layer, tm, t, b_off),
            pl.BlockSpec((None, None, 1, D_MODEL), lambda i, j: (layer, 1, 0, 0)),
            pl.BlockSpec((None, D_MODEL, PROJ_TILE), lambda i, j: (layer, 0, j)),
        ],
        out_specs=pl.BlockSpec((tm, PROJ_TILE), lambda i, j: (i, j)),
        out_shape=jax.ShapeDtypeStruct((n, PROJ_WIDTH), F32),
        scratch_shapes=[pltpu.VMEM((tm, D_MODEL), BF16)],
        compiler_params=pltpu.CompilerParams(
            dimension_semantics=("arbitrary", "arbitrary"), vmem_limit_bytes=WIDE_VMEM_LIMIT),
        name="in_proj",
    )(x, mod, norm_g, w_in)


def _split3(x):
    hi = x.astype(BF16)
    r1 = x - hi.astype(F32)
    mid = r1.astype(BF16)
    lo = (r1 - mid.astype(F32)).astype(BF16)
    return hi, mid, lo


def _wkv_kernel(p_ref, sh0_ref, s0_ref, mu_ref, vec_ref, wlr_ref, o_ref, sout_ref, state_ref, prev_ref, *, chunk):
    c = pl.program_id(1)
    C = chunk
    C2 = 2 * C
    n_streams = p_ref.shape[1]

    @pl.when(c == 0)
    def _():
        state_ref[...] = s0_ref[0]
        prev_ref[...] = sh0_ref[0]

    w0, a0, k_k, k_a, lnx_w, lnx_b, r_k = (vec_ref[i:i + 1, :] for i in range(7))
    lane = lax.broadcasted_iota(jnp.int32, (1, LANES), 1)
    head0 = lane < HEAD_SIZE
    ri = lax.broadcasted_iota(jnp.int32, (LANES, LANES), 0)
    ci = lax.broadcasted_iota(jnp.int32, (LANES, LANES), 1)
    head_ones = ((ri < HEAD_SIZE) == (ci < HEAD_SIZE)).astype(BF16)
    tr = lax.broadcasted_iota(jnp.int32, (C, C), 0)
    tc = lax.broadcasted_iota(jnp.int32, (C, C), 1)
    cum_ones = (tc <= tr).astype(BF16)
    row = lax.broadcasted_iota(jnp.int32, (C, 1), 0)

    def head_sum(x):
        st = jnp.concatenate([x[:, g * LANES:(g + 1) * LANES] for g in range(N_PAIRS)], axis=0)
        s = _dot(st.astype(BF16), head_ones)
        return jnp.concatenate([s[g * C:(g + 1) * C] for g in range(N_PAIRS)], axis=1)

    def prepare(s):
        p = p_ref[0, s]
        prev = jnp.where(row == 0, prev_ref[s], pltpu.roll(p, 1, 0))
        prev_ref[s] = p[C - 1:C, :]
        xs = p + (prev - p) * mu_ref[...]

        r = xs[:, 0:RWKV_WIDTH]
        k = xs[:, RWKV_WIDTH:2 * RWKV_WIDTH]
        v = xs[:, 2 * RWKV_WIDTH:3 * RWKV_WIDTH]
        z = xs[:, 3 * RWKV_WIDTH:RWKV_COLS]
        zc = lax.broadcasted_iota(jnp.int32, z.shape, 1)
        zact = jnp.where(zc < DECAY_RANK, jnp.tanh(z),
                         jnp.where(zc < DECAY_RANK + A_RANK, z, jax.nn.sigmoid(z)))
        zact = zact.astype(BF16)
        n_da = DECAY_RANK + A_RANK
        lr = _dot(zact[:, :n_da], wlr_ref[:n_da, :2 * RWKV_WIDTH])
        gate = _dot(zact[:, n_da:], wlr_ref[n_da:, 2 * RWKV_WIDTH:])

        zz = -(w0 + lr[:, 0:RWKV_WIDTH])
        softplus = jnp.maximum(zz, 0.0) + jnp.log(1.0 + jnp.exp(-jnp.abs(zz)))
        lw = -jnp.exp(-softplus - 0.5)
        a = jax.nn.sigmoid(a0 + lr[:, RWKV_WIDTH:2 * RWKV_WIDTH])

        hi, mid, lo = _split3(lw)
        cs = _dot(cum_ones, hi) + _dot(cum_ones, mid) + _dot(cum_ones, lo)
        w_in = jnp.exp(cs)
        w_inv = jnp.exp(-cs)
        w_ex = jnp.exp(cs - lw)

        kk = k * k_k
        k2 = k * (1.0 + (a - 1.0) * k_a)
        kk = kk * lax.rsqrt(jnp.maximum(head_sum(kk * kk), 1e-24))
        b = kk * a
        k_h = k2 * w_inv
        b_h = b * w_inv
        w_c = w_in[C - 1:C, :]
        return dict(v=v, r_t=r * w_in, kk_t=kk * w_ex, k_h=k_h, b_h=b_h, w_c=w_c, k_b=k_h * w_c, b_b=b_h * w_c,
                    gate=gate, bonus=head_sum(r * k2 * r_k) * v)

    pre = [prepare(s) for s in range(n_streams)]

    tr2 = lax.broadcasted_iota(jnp.int32, (C, C2), 0)
    tc2 = lax.broadcasted_iota(jnp.int32, (C, C2), 1)
    col_head0 = tc2 < C
    ts2 = jnp.where(col_head0, tc2, tc2 - C)
    strict = ts2 < tr2
    incl = ts2 <= tr2

    def stack(x):
        xb = x.astype(BF16)
        zero = jnp.zeros_like(xb)
        return jnp.concatenate([jnp.where(head0, xb, zero), jnp.where(head0, zero, xb)], axis=0)

    def block_diag(p):
        pb = p.astype(BF16)
        zero = jnp.zeros_like(pb)
        return jnp.concatenate([jnp.where(col_head0, pb, zero), jnp.where(col_head0, zero, pb)], axis=0)

    chains = [(s, g) for s in range(n_streams) for g in range(N_PAIRS)]
    pairs = range(len(chains))

    def lanes(name, i):
        s, g = chains[i]
        return pre[s][name][:, g * LANES:(g + 1) * LANES]

    s_prev = [state_ref[s, g] for s, g in chains]
    q = [jnp.concatenate([lanes("kk_t", i), lanes("r_t", i)], axis=0).astype(BF16) for i in pairs]
    kb = [jnp.concatenate([stack(lanes("k_h", i)), stack(lanes("b_h", i))], axis=0) for i in pairs]
    s_v = [stack(lanes("v", i)) for i in pairs]
    kb_c = [jnp.concatenate([stack(lanes("k_b", i)), -stack(lanes("b_b", i))], axis=0) for i in pairs]

    scores = [_dot_nt(q[g], kb[g]) for g in pairs]
    proj = [_dot_nt(q[g], s_prev[g].astype(BF16)) for g in pairs]
    a_kk_k = [jnp.where(strict, s[:C, :C2], 0.0).astype(BF16) for s in scores]
    m_b = [jnp.where(strict, s[:C, C2:], 0.0) for s in scores]
    a_r = [jnp.concatenate([jnp.where(incl, s[C:, :C2], 0.0), jnp.where(incl, -s[C:, C2:], 0.0)],
                           axis=1).astype(BF16) for s in scores]
    z = [_dot(a_kk_k[g], s_v[g]) + proj[g][:C] for g in pairs]

    pw = m_b
    sign = -1.0
    n = 1
    while n < C:
        last = 2 * n >= C
        nxt = []
        for g in pairs:
            if last:
                pz = _dot(pw[g].astype(BF16), stack(z[g]))
            else:
                both = _dot(pw[g].astype(BF16), jnp.concatenate([block_diag(pw[g]), stack(z[g])], axis=1))
                nxt.append(both[:, :C2])
                pz = both[:, C2:]
            z[g] = z[g] + sign * pz
        pw = nxt
        sign = 1.0
        n *= 2

    v_sa = [jnp.concatenate([s_v[g], stack(z[g])], axis=0) for g in pairs]
    ys = []
    for i in pairs:
        s, g = chains[i]
        ys.append(proj[i][C:] + _dot(a_r[i], v_sa[i]))
        state_ref[s, g] = s_prev[i] * lanes("w_c", i) + _dot_tn(v_sa[i], kb_c[i])

    for s in range(n_streams):
        y = jnp.concatenate(ys[s * N_PAIRS:(s + 1) * N_PAIRS], axis=1)
        mean = head_sum(y) * (1.0 / HEAD_SIZE)
        d = y - mean
        var = head_sum(d * d) * (1.0 / HEAD_SIZE)
        yn = d * lax.rsqrt(var + GN_EPS) * lnx_w + lnx_b + pre[s]["bonus"]
        o_ref[0, s] = (yn * pre[s]["gate"]).astype(BF16)

    @pl.when(c == pl.num_programs(1) - 1)
    def _():
        sout_ref[0] = state_ref[...]


WKV_STREAMS = 4


def _wkv(proj, shift0, s0_pairs, mu, vecs, w_lowrank, *, layer, nb, t, chunk):
    ns = WKV_STREAMS
    nc = t // chunk
    kern = functools.partial(_wkv_kernel, chunk=chunk)
    grouped = lambda a: a.reshape((nb // ns, ns) + a.shape[1:])
    yg, s_new = pl.pallas_call(
        kern,
        grid=(nb // ns, nc),
        in_specs=[
            pl.BlockSpec((1, ns, chunk, RWKV_COLS), lambda b, c: (b, 0, c, 0)),
            pl.BlockSpec((1, ns, 1, RWKV_COLS), lambda b, c: (b, 0, 0, 0)),
            pl.BlockSpec((1, ns, N_PAIRS, LANES, LANES), lambda b, c: (b, 0, 0, 0, 0)),
            pl.BlockSpec((None, 1, RWKV_COLS), lambda b, c: (layer, 0, 0)),
            pl.BlockSpec((None, 8, RWKV_WIDTH), lambda b, c: (layer, 0, 0)),
            pl.BlockSpec((None, LOWRANK_PAD, 3 * RWKV_WIDTH), lambda b, c: (layer, 0, 0)),
        ],
        out_specs=[
            pl.BlockSpec((1, ns, chunk, RWKV_WIDTH), lambda b, c: (b, 0, c, 0)),
            pl.BlockSpec((1, ns, N_PAIRS, LANES, LANES), lambda b, c: (b, 0, 0, 0, 0)),
        ],
        out_shape=[
            jax.ShapeDtypeStruct((nb // ns, ns, t, RWKV_WIDTH), BF16),
            jax.ShapeDtypeStruct((nb // ns, ns, N_PAIRS, LANES, LANES), F32),
        ],
        scratch_shapes=[pltpu.VMEM((ns, N_PAIRS, LANES, LANES), F32), pltpu.VMEM((ns, 1, RWKV_COLS), F32)],
        compiler_params=pltpu.CompilerParams(
            dimension_semantics=("arbitrary", "arbitrary"), vmem_limit_bytes=VMEM_LIMIT),
        name="wkv",
    )(proj.reshape(nb // ns, ns, t, PROJ_WIDTH), grouped(shift0), grouped(s0_pairs), mu, vecs, w_lowrank)
    return yg.reshape(nb * t, RWKV_WIDTH), s_new.reshape(nb, N_PAIRS, LANES, LANES)


def _merge_kernel(x_ref, m_ref, yg_ref, pp_ref, buf_ref, gt_ref, wo_ref, wp_ref, ps_ref, wout_ref, o_ref, carry_ref,
                  *, pos0, tm):
    i = pl.program_id(1)

    @pl.when(i == 0)
    def _():
        carry_ref[...] = buf_ref[0]

    xp = pp_ref[...]
    full = jnp.concatenate([carry_ref[...], xp], axis=0)
    carry_ref[...] = full[tm:tm + POOL_CARRY]

    pos = pos0 + i * tm + lax.broadcasted_iota(jnp.int32, (tm, POOL_GROUP), 0)
    o_b = []
    for gi, win in enumerate(POOL_WINDOWS):
        cols = slice(gi * POOL_GROUP, (gi + 1) * POOL_GROUP)
        s = full[:, cols]
        width = 1
        while width < win:
            s = s[width:] + s[:-width]
            width *= 2
        s = s[POOL_CARRY - (win - 1):]
        cnt = jnp.minimum(win, pos + 1).astype(F32)
        pooled = s / cnt - xp[:, cols]
        o_b.append(_dot(pooled.astype(BF16), wp_ref[gi]))
    o_b = jnp.concatenate(o_b, axis=1) * ps_ref[...]
    o_a = _dot(yg_ref[...], wo_ref[...])
    gates = jax.nn.sigmoid(gt_ref[...])
    merged = gates[:, :D_MODEL] * o_a + gates[:, D_MODEL:] * o_b
    o_ref[...] = x_ref[...] + m_ref[...][:, 5, :] * _dot(merged.astype(BF16), wout_ref[...])


def _merge(x, mod, yg, proj, pool0, w_o, w_pool, pool_scale, w_out, *, layer, nb, t, b_off, tm, pos0):
    n = x.shape[0]
    nt = t // tm
    kern = functools.partial(_merge_kernel, pos0=pos0, tm=tm)
    row = lambda b, i: (b * nt + i, 0)
    return pl.pallas_call(
        kern,
        grid=(nb, nt),
        in_specs=[
            pl.BlockSpec((tm, D_MODEL), row),
            pl.BlockSpec((None, 1, N_MOD, D_MODEL), lambda b, i: (layer, b_off + b, 0, 0)),
            pl.BlockSpec((tm, RWKV_WIDTH), row),
            pl.BlockSpec((tm, POOL_WIDTH), lambda b, i: (b * nt + i, POOL_COL // POOL_WIDTH)),
            pl.BlockSpec((1, POOL_CARRY, POOL_WIDTH), lambda b, i: (b, 0, 0)),
            pl.BlockSpec((tm, 2 * D_MODEL), lambda b, i: (b * nt + i, GATE_COL // (2 * D_MODEL))),
            pl.BlockSpec((None, RWKV_WIDTH, D_MODEL), lambda b, i: (layer, 0, 0), pipeline_mode=pl.Buffered(1)),
            pl.BlockSpec((None, len(POOL_WINDOWS), POOL_GROUP, POOL_OUT_GROUP), lambda b, i: (layer, 0, 0, 0),
                         pipeline_mode=pl.Buffered(1)),
            pl.BlockSpec((None, 1, D_MODEL), lambda b, i: (layer, 0, 0)),
            pl.BlockSpec((None, D_MODEL, D_MODEL), lambda b, i: (layer, 0, 0), pipeline_mode=pl.Buffered(1)),
        ],
        out_specs=pl.BlockSpec((tm, D_MODEL), row),
        out_shape=jax.ShapeDtypeStruct((n, D_MODEL), F32),
        scratch_shapes=[pltpu.VMEM((POOL_CARRY, POOL_WIDTH), F32)],
        compiler_params=pltpu.CompilerParams(
            dimension_semantics=("arbitrary", "arbitrary"), vmem_limit_bytes=WIDE_VMEM_LIMIT),
        name="merge",
    )(x, mod, yg, proj, pool0, proj, w_o, w_pool, pool_scale, w_out)


def _to_pairs(s):
    b = s.shape[0]
    s = s.reshape(b, N_PAIRS, 2, HEAD_SIZE, HEAD_SIZE)
    z = jnp.zeros_like(s[:, :, 0])
    top = jnp.concatenate([s[:, :, 0], z], axis=-1)
    bot = jnp.concatenate([z, s[:, :, 1]], axis=-1)
    return jnp.concatenate([top, bot], axis=-2)


def _from_pairs(sp):
    b = sp.shape[0]
    h0 = sp[:, :, :HEAD_SIZE, :HEAD_SIZE]
    h1 = sp[:, :, HEAD_SIZE:, HEAD_SIZE:]
    return jnp.stack([h0, h1], axis=2).reshape(b, N_HEADS, HEAD_SIZE, HEAD_SIZE)


def _stream_set(x, t, nb, b_off, *, ffn_tm, proj_tm, merge_tm, chunk, pos0):
    return dict(x=x.reshape(nb * t, D_MODEL), t=t, nb=nb, b_off=b_off, ffn_tm=ffn_tm, proj_tm=proj_tm,
                merge_tm=merge_tm, chunk=chunk, pos0=pos0)


def kernel(x_prompt, x_sample, c_prompt, c_sample, state_wkv, state_shift, state_pool, norm_g, w_mod, b_mod,
           ffn_w1, ffn_w3, ffn_w2, w_in, shift_mu, w0, w_decay, a0, w_a, w_gate, k_k, k_a, r_k, lnx_w, lnx_b,
           w_o_rwkv, w_pool, pool_scale, w_out, final_g):
    bp, seq, _ = x_prompt.shape
    bs, dseq, _ = x_sample.shape
    past_len = 1024
    dt = x_prompt.dtype

    pad = jnp.zeros((DEPTH, D_MODEL, GATE_COL - SHIFT_WIDTH), BF16)
    w_in_b = w_in.astype(BF16)
    w_in_p = jnp.concatenate([w_in_b[:, :, :SHIFT_WIDTH], pad, w_in_b[:, :, SHIFT_WIDTH + POOL_WIDTH:],
                              w_in_b[:, :, SHIFT_WIDTH:SHIFT_WIDTH + POOL_WIDTH]], axis=2)
    w_o_b = w_o_rwkv.astype(BF16)
    w_pool_b = w_pool.astype(BF16)
    w_out_b = w_out.astype(BF16)
    zr = lambda rows: jnp.zeros((DEPTH, rows, RWKV_WIDTH), F32)
    w_lr = jnp.concatenate([
        jnp.concatenate([w_decay, zr(LOWRANK_PAD - DECAY_RANK)], axis=1),
        jnp.concatenate([zr(DECAY_RANK), w_a, zr(LOWRANK_PAD - DECAY_RANK - A_RANK)], axis=1),
        jnp.concatenate([zr(DECAY_RANK + A_RANK), w_gate, zr(LOWRANK_PAD - LOWRANK)], axis=1),
    ], axis=2).astype(BF16)
    vecs = jnp.stack([w0, a0, k_k, k_a, lnx_w, lnx_b, r_k.reshape(DEPTH, RWKV_WIDTH),
                      jnp.zeros_like(w0)], axis=1)
    mu = jnp.pad(shift_mu, ((0, 0), (0, RWKV_COLS - SHIFT_WIDTH))).reshape(DEPTH, 1, RWKV_COLS)
    norm_g4 = norm_g.reshape(DEPTH, 3, 1, D_MODEL)
    fg = final_g.reshape(1, D_MODEL)
    ps = pool_scale.reshape(DEPTH, 1, D_MODEL)

    mod = _modulation(jnp.concatenate([c_prompt, c_sample], axis=0), w_mod, b_mod)
    mod = mod.reshape(DEPTH, bp + bs, N_MOD, D_MODEL)

    prompt = _stream_set(x_prompt, seq, bp, 0, ffn_tm=min(1024, seq), proj_tm=min(1024, seq),
                         merge_tm=min(512, seq), chunk=min(64, seq), pos0=0)
    sample = _stream_set(x_sample, dseq, bs, bp, ffn_tm=bs * dseq, proj_tm=bs * dseq, merge_tm=dseq,
                         chunk=dseq, pos0=past_len)

    outs = {"p": ([], [], []), "s": ([], [], [])}
    ffn_bf16 = {}
    for l in range(DEPTH):
        for tag, st in (("s", sample), ("p", prompt)):
            nb, t = st["nb"], st["t"]
            if tag == "p":
                shift0 = jnp.zeros((nb, 1, RWKV_COLS), F32)
                s0 = jnp.zeros((nb, N_PAIRS, LANES, LANES), F32)
                pool0 = jnp.zeros((nb, POOL_CARRY, POOL_WIDTH), F32)
            else:
                shift0 = jnp.pad(state_shift[l], ((0, 0), (0, RWKV_COLS - SHIFT_WIDTH))).reshape(nb, 1, RWKV_COLS)
                s0 = _to_pairs(state_wkv[l].astype(F32))
                pool0 = jnp.pad(state_pool[l], ((0, 0), (POOL_CARRY - POOL_BUF, 0), (0, 0)))
            common = dict(layer=l, t=t, b_off=st["b_off"])

            def ffn(x, which, sub, final):
                args = dict(which=which, sub=sub, tm=st["ffn_tm"], final=final, **common)
                if tag == "s":
                    x, *ffn_bf16[l, which] = _ffn(x, mod, norm_g4, ffn_w1, ffn_w3, ffn_w2, fg, emit_bf16=True, **args)
                    return x
                return _ffn(x, mod, norm_g4, *ffn_bf16[l, which], fg, **args)

            x = st["x"]
            x = ffn(x, 0, 0, False)
            proj = _in_proj(x, mod, norm_g4, w_in_p, tm=st["proj_tm"], **common)
            yg, s_new = _wkv(proj, shift0, s0, mu, vecs, w_lr, layer=l, nb=nb, t=t, chunk=st["chunk"])
            x = _merge(x, mod, yg, proj, pool0, w_o_b, w_pool_b, ps, w_out_b, nb=nb,
                       tm=st["merge_tm"], pos0=st["pos0"], **common)
            x = ffn(x, 1, 2, l == DEPTH - 1)
            st["x"] = x
            wkv_l, shift_l, pool_l = outs[tag]
            wkv_l.append(_from_pairs(s_new).astype(dt))
            proj3 = proj.reshape(nb, t, PROJ_WIDTH)
            shift_l.append(proj3[:, -1, :SHIFT_WIDTH])
            pool_l.append(proj3[:, t - POOL_BUF:, POOL_COL:])

    yp = prompt["x"].reshape(bp, seq, D_MODEL)
    ys = sample["x"].reshape(bs, dseq, D_MODEL)
    pw, psh, ppl = (jnp.stack(v) for v in outs["p"])
    sw, ssh, spl = (jnp.stack(v) for v in outs["s"])
    return (yp, ys, pw, psh, ppl, sw, ssh, spl)
```

```python
import functools

import jax
import jax.numpy as jnp
from jax import lax
from jax.experimental import pallas as pl
from jax.experimental.pallas import tpu as pltpu

F32 = jnp.float32
BF16 = jnp.bfloat16

D_MODEL = 2048
DEPTH = 4
RWKV_WIDTH = D_MODEL // 2
HEAD_SIZE = 64
N_HEADS = RWKV_WIDTH // HEAD_SIZE
DECAY_RANK = 64
A_RANK = 64
GATE_RANK = 160
POOL_WIDTH = D_MODEL // 2
POOL_WINDOWS = (2, 4, 8, 16)
POOL_GROUP = POOL_WIDTH // len(POOL_WINDOWS)
POOL_OUT_GROUP = D_MODEL // len(POOL_WINDOWS)
POOL_BUF = max(POOL_WINDOWS) - 1
D_FF = ((8 * D_MODEL // 3 + 255) // 256) * 256
SHIFT_WIDTH = 3 * RWKV_WIDTH + DECAY_RANK + A_RANK + GATE_RANK
N_MOD = 9
RMS_EPS = 1e-6
GN_EPS = 64e-5

LANES = 128
N_PAIRS = RWKV_WIDTH // LANES
LOWRANK = DECAY_RANK + A_RANK + GATE_RANK
LOWRANK_PAD = 3 * LANES
RWKV_COLS = 3 * RWKV_WIDTH + LOWRANK_PAD
PROJ_TILE = 2304
GATE_COL = 2 * D_MODEL
POOL_COL = 4 * D_MODEL
PROJ_WIDTH = POOL_COL + POOL_WIDTH
POOL_CARRY = 16
PAST_LEN = 1024

FFN_ROWS = 1024
PROJ_ROWS = 1024
MERGE_ROWS = 512
WKV_CHUNK = 64
V7X_VMEM_BYTES = 64 * 1024 * 1024
VMEM_LIMIT = V7X_VMEM_BYTES - 8 * 1024 * 1024
WIDE_VMEM_LIMIT = V7X_VMEM_BYTES - 2 * 1024 * 1024


def _dot(a, b):
    return jnp.dot(a, b, preferred_element_type=F32)


def _dot_nt(a, b):
    return lax.dot_general(a, b, (((1,), (1,)), ((), ())), preferred_element_type=F32)


def _dot_tn(a, b):
    return lax.dot_general(a, b, (((0,), (0,)), ((), ())), preferred_element_type=F32)


def _rows(v, tm):
    g, d = v.shape
    if g == 1:
        return v
    return jnp.broadcast_to(v[:, None, :], (g, tm // g, d)).reshape(tm, d)


NORM_ROWS = 16


ROW_SPLIT = 4


def _modulated_norm_blocks(h_ref, x_ref, g_ref, m_ref, sub):
    tm = x_ref.shape[0]
    rows_per_stream = tm // m_ref.shape[0]
    block = tm // ROW_SPLIT
    gain = g_ref[...]
    cache = {}

    def scale_shift(s):
        if s not in cache:
            cache[s] = (gain * (1.0 + m_ref[s, 3 * sub + 1:3 * sub + 2, :]), m_ref[s, 3 * sub:3 * sub + 1, :])
        return cache[s]

    def store(r):
        for r0 in range(r * block, (r + 1) * block, NORM_ROWS):
            gs, shift = scale_shift(r0 // rows_per_stream)
            x = x_ref[r0:r0 + NORM_ROWS, :]
            y = x * lax.rsqrt(jnp.mean(x * x, axis=-1, keepdims=True) + RMS_EPS)
            h_ref[r0:r0 + NORM_ROWS, :] = (y * gs + shift).astype(BF16)

    return store, block


def _mod_kernel(c_ref, w_ref, b_ref, o_ref):
    c = c_ref[...]
    s = (c * jax.nn.sigmoid(c)).astype(BF16)
    o_ref[0] = _dot(s, w_ref[0].astype(BF16)) + b_ref[0]


def _modulation(c_all, w_mod, b_mod):
    nb = c_all.shape[0]
    width = N_MOD * D_MODEL
    tn = 1024
    return pl.pallas_call(
        _mod_kernel,
        grid=(DEPTH, width // tn),
        in_specs=[
            pl.BlockSpec((nb, D_MODEL), lambda l, j: (0, 0)),
            pl.BlockSpec((1, D_MODEL, tn), lambda l, j: (l, 0, j)),
            pl.BlockSpec((1, 1, tn), lambda l, j: (l, 0, j)),
        ],
        out_specs=pl.BlockSpec((1, nb, tn), lambda l, j: (l, 0, j)),
        out_shape=jax.ShapeDtypeStruct((DEPTH, nb, width), F32),
        compiler_params=pltpu.CompilerParams(
            dimension_semantics=("arbitrary", "arbitrary"), vmem_limit_bytes=VMEM_LIMIT),
        name="modulation",
    )(c_all, w_mod, b_mod.reshape(DEPTH, 1, width))


def _mod_spec(layer, tm, t, b_off):
    g = max(1, tm // t)
    tiles_per_stream = max(1, t // tm)
    assert b_off % g == 0 and (tm % t == 0 or t % tm == 0)
    base = b_off // g
    return pl.BlockSpec((None, g, N_MOD, D_MODEL),
                        lambda i, j: (layer, base + i // tiles_per_stream, 0, 0))


FF_TILE = 512


def _ffn_kernel(x_ref, m_ref, g_ref, w1_ref, w3_ref, w2_ref, fg_ref, o_ref, *rest, sub, final, emit_bf16):
    if emit_bf16:
        w1b_ref, w3b_ref, w2b_ref, h_ref = rest
        w1b_ref[...] = w1_ref[...].astype(BF16)
        w3b_ref[...] = w3_ref[...].astype(BF16)
        w2b_ref[...] = w2_ref[...].astype(BF16)
        w1_ref, w3_ref, w2_ref = w1b_ref, w3b_ref, w2b_ref
    else:
        (h_ref,) = rest
    j = pl.program_id(1)
    last = pl.num_programs(1) - 1
    tm = x_ref.shape[0]

    def gated(h):
        a = _dot(h, w1_ref[...])
        return (a * jax.nn.sigmoid(a) * _dot(h, w3_ref[...])).astype(BF16)

    @pl.when(j == 0)
    def _():
        store_norm, block = _modulated_norm_blocks(h_ref, x_ref, g_ref, m_ref, sub)
        store_norm(0)
        for r in range(ROW_SPLIT):
            if r + 1 < ROW_SPLIT:
                store_norm(r + 1)
            rows = slice(r * block, (r + 1) * block)
            o_ref[rows, :] = _dot(gated(h_ref[rows, :]), w2_ref[...])

    @pl.when(jnp.logical_and(j > 0, j < last))
    def _():
        o_ref[...] += _dot(gated(h_ref[...]), w2_ref[...])

    @pl.when(j == last)
    def _():
        block = tm // ROW_SPLIT
        u = gated(h_ref[...])
        gate = 0.5 * _rows(m_ref[...][:, 3 * sub + 2, :], tm)
        for r in range(ROW_SPLIT):
            rows = slice(r * block, (r + 1) * block)
            acc = o_ref[rows, :] + _dot(u[rows, :], w2_ref[...])
            y = x_ref[rows, :] + (gate if gate.shape[0] == 1 else gate[rows, :]) * acc
            if final:
                y = y * lax.rsqrt(jnp.mean(y * y, axis=-1, keepdims=True) + RMS_EPS) * fg_ref[...]
            o_ref[rows, :] = y


def _ffn(x, mod, norm_g, w1, w3, w2, final_g, *, layer, which, sub, t, b_off, tm, final, emit_bf16=False):
    n = x.shape[0]
    tf = FF_TILE
    assert D_FF // tf >= 2
    kern = functools.partial(_ffn_kernel, sub=sub, final=final, emit_bf16=emit_bf16)
    w13_tile = pl.BlockSpec((D_MODEL, tf), lambda i, j: (0, j))
    w2_tile = pl.BlockSpec((tf, D_MODEL), lambda i, j: (j, 0))
    out_specs = pl.BlockSpec((tm, D_MODEL), lambda i, j: (i, 0))
    out_shape = jax.ShapeDtypeStruct((n, D_MODEL), F32)
    if emit_bf16:
        assert n == tm
        w_specs = [
            pl.BlockSpec((None, None, D_MODEL, tf), lambda i, j: (layer, which, 0, j)),
            pl.BlockSpec((None, None, D_MODEL, tf), lambda i, j: (layer, which, 0, j)),
            pl.BlockSpec((None, None, tf, D_MODEL), lambda i, j: (layer, which, j, 0)),
        ]
        out_specs = [out_specs, w13_tile, w13_tile, w2_tile]
        out_shape = [out_shape, jax.ShapeDtypeStruct((D_MODEL, D_FF), BF16),
                     jax.ShapeDtypeStruct((D_MODEL, D_FF), BF16), jax.ShapeDtypeStruct((D_FF, D_MODEL), BF16)]
    else:
        w_specs = [w13_tile, w13_tile, w2_tile]
    return pl.pallas_call(
        kern,
        grid=(n // tm, D_FF // tf),
        in_specs=[
            pl.BlockSpec((tm, D_MODEL), lambda i, j: (i, 0)),
            _mod_spec(layer, tm, t, b_off),
            pl.BlockSpec((None, None, 1, D_MODEL), lambda i, j: (layer, sub, 0, 0)),
            *w_specs,
            pl.BlockSpec((1, D_MODEL), lambda i, j: (0, 0)),
        ],
        out_specs=out_specs,
        out_shape=out_shape,
        scratch_shapes=[pltpu.VMEM((tm, D_MODEL), BF16)],
        compiler_params=pltpu.CompilerParams(
            dimension_semantics=("arbitrary", "arbitrary"), vmem_limit_bytes=WIDE_VMEM_LIMIT),
        name="ffn",
    )(x, mod, norm_g, w1, w3, w2, final_g)


def _proj_kernel(x_ref, m_ref, g_ref, w_ref, o_ref, h_ref):
    j = pl.program_id(1)

    @pl.when(j == 0)
    def _():
        store_norm, block = _modulated_norm_blocks(h_ref, x_ref, g_ref, m_ref, 1)
        store_norm(0)
        for r in range(ROW_SPLIT):
            if r + 1 < ROW_SPLIT:
                store_norm(r + 1)
            rows = slice(r * block, (r + 1) * block)
            o_ref[rows, :] = _dot(h_ref[rows, :], w_ref[...])

    @pl.when(j > 0)
    def _():
        o_ref[...] = _dot(h_ref[...], w_ref[...])


def _in_proj(x, mod, norm_g, w_in, *, layer, t, b_off, tm):
    n = x.shape[0]
    return pl.pallas_call(
        _proj_kernel,
        grid=(n // tm, PROJ_WIDTH // PROJ_TILE),
        in_specs=[
            pl.BlockSpec((tm, D_MODEL), lambda i, j: (i, 0)),
            _mod_spec(layer, tm, t, b_off),
            pl.BlockSpec((None, None, 1, D_MODEL), lambda i, j: (layer, 1, 0, 0)),
            pl.BlockSpec((None, D_MODEL, PROJ_TILE), lambda i, j: (layer, 0, j)),
        ],
        out_specs=pl.BlockSpec((tm, PROJ_TILE), lambda i, j: (i, j)),
        out_shape=jax.ShapeDtypeStruct((n, PROJ_WIDTH), F32),
        scratch_shapes=[pltpu.VMEM((tm, D_MODEL), BF16)],
        compiler_params=pltpu.CompilerParams(
            dimension_semantics=("arbitrary", "arbitrary"), vmem_limit_bytes=WIDE_VMEM_LIMIT),
        name="in_proj",
    )(x, mod, norm_g, w_in)


def _split3(x):
    hi = x.astype(BF16)
    r1 = x - hi.astype(F32)
    mid = r1.astype(BF16)
    lo = (r1 - mid.astype(F32)).astype(BF16)
    return hi, mid, lo


def _wkv_kernel(p_ref, sh0_ref, s0_ref, mu_ref, vec_ref, wlr_ref, o_ref, sout_ref, state_ref, prev_ref, *, chunk):
    c = pl.program_id(1)
    C = chunk
    C2 = 2 * C
    n_streams = p_ref.shape[1]

    @pl.when(c == 0)
    def _():
        state_ref[...] = s0_ref[0]
        prev_ref[...] = sh0_ref[0]

    w0, a0, k_k, k_a, lnx_w, lnx_b, r_k = (vec_ref[i:i + 1, :] for i in range(7))
    lane = lax.broadcasted_iota(jnp.int32, (1, LANES), 1)
    head0 = lane < HEAD_SIZE
    ri = lax.broadcasted_iota(jnp.int32, (LANES, LANES), 0)
    ci = lax.broadcasted_iota(jnp.int32, (LANES, LANES), 1)
    head_ones = ((ri < HEAD_SIZE) == (ci < HEAD_SIZE)).astype(BF16)
    tr = lax.broadcasted_iota(jnp.int32, (C, C), 0)
    tc = lax.broadcasted_iota(jnp.int32, (C, C), 1)
    cum_ones = (tc <= tr).astype(BF16)
    row = lax.broadcasted_iota(jnp.int32, (C, 1), 0)

    def head_sum(x):
        st = jnp.concatenate([x[:, g * LANES:(g + 1) * LANES] for g in range(N_PAIRS)], axis=0)
        s = _dot(st.astype(BF16), head_ones)
        return jnp.concatenate([s[g * C:(g + 1) * C] for g in range(N_PAIRS)], axis=1)

    def prepare(s):
        p = p_ref[0, s]
        prev = jnp.where(row == 0, prev_ref[s], pltpu.roll(p, 1, 0))
        prev_ref[s] = p[C - 1:C, :]
        xs = p + (prev - p) * mu_ref[...]
        yield

        r = xs[:, 0:RWKV_WIDTH]
        k = xs[:, RWKV_WIDTH:2 * RWKV_WIDTH]
        v = xs[:, 2 * RWKV_WIDTH:3 * RWKV_WIDTH]
        z = xs[:, 3 * RWKV_WIDTH:RWKV_COLS]
        zc = lax.broadcasted_iota(jnp.int32, z.shape, 1)
        zact = jnp.where(zc < DECAY_RANK, jnp.tanh(z),
                         jnp.where(zc < DECAY_RANK + A_RANK, z, jax.nn.sigmoid(z)))
        zact = zact.astype(BF16)
        n_da = DECAY_RANK + A_RANK
        lr = _dot(zact[:, :n_da], wlr_ref[:n_da, :2 * RWKV_WIDTH])
        gate = _dot(zact[:, n_da:], wlr_ref[n_da:, 2 * RWKV_WIDTH:])
        yield

        zz = -(w0 + lr[:, 0:RWKV_WIDTH])
        softplus = jnp.maximum(zz, 0.0) + jnp.log(1.0 + jnp.exp(-jnp.abs(zz)))
        lw = -jnp.exp(-softplus - 0.5)
        a = jax.nn.sigmoid(a0 + lr[:, RWKV_WIDTH:2 * RWKV_WIDTH])
        yield

        hi, mid, lo = _split3(lw)
        cs = _dot(cum_ones, hi) + _dot(cum_ones, mid) + _dot(cum_ones, lo)
        yield

        w_in = jnp.exp(cs)
        w_inv = jnp.exp(-cs)
        w_ex = jnp.exp(cs - lw)
        yield

        kk = k * k_k
        k2 = k * (1.0 + (a - 1.0) * k_a)
        kk = kk * lax.rsqrt(jnp.maximum(head_sum(kk * kk), 1e-24))
        yield

        b = kk * a
        k_h = k2 * w_inv
        b_h = b * w_inv
        w_c = w_in[C - 1:C, :]
        yield

        pre[s] = dict(v=v, r_t=r * w_in, kk_t=kk * w_ex, k_h=k_h, b_h=b_h, w_c=w_c, k_b=k_h * w_c, b_b=b_h * w_c,
                      gate=gate, bonus=head_sum(r * k2 * r_k) * v)

    pre = {}

    tr2 = lax.broadcasted_iota(jnp.int32, (C, C2), 0)
    tc2 = lax.broadcasted_iota(jnp.int32, (C, C2), 1)
    col_head0 = tc2 < C
    ts2 = jnp.where(col_head0, tc2, tc2 - C)
    strict = ts2 < tr2
    incl = ts2 <= tr2

    def stack(x):
        xb = x.astype(BF16)
        zero = jnp.zeros_like(xb)
        return jnp.concatenate([jnp.where(head0, xb, zero), jnp.where(head0, zero, xb)], axis=0)

    def block_diag(p):
        pb = p.astype(BF16)
        zero = jnp.zeros_like(pb)
        return jnp.concatenate([jnp.where(col_head0, pb, zero), jnp.where(col_head0, zero, pb)], axis=0)

    def advance(streams, tick):
        chains = [(s, g) for s in streams for g in range(N_PAIRS)]
        pairs = range(len(chains))

        def lanes(name, i):
            s, g = chains[i]
            return pre[s][name][:, g * LANES:(g + 1) * LANES]

        s_prev = [state_ref[s, g] for s, g in chains]
        q = [jnp.concatenate([lanes("kk_t", i), lanes("r_t", i)], axis=0).astype(BF16) for i in pairs]
        kb = [jnp.concatenate([stack(lanes("k_h", i)), stack(lanes("b_h", i))], axis=0) for i in pairs]
        s_v = [stack(lanes("v", i)) for i in pairs]
        kb_c = [jnp.concatenate([stack(lanes("k_b", i)), -stack(lanes("b_b", i))], axis=0) for i in pairs]

        scores = [_dot_nt(q[g], kb[g]) for g in pairs]
        proj = [_dot_nt(q[g], s_prev[g].astype(BF16)) for g in pairs]
        tick()
        a_kk_k = [jnp.where(strict, s[:C, :C2], 0.0).astype(BF16) for s in scores]
        m_b = [jnp.where(strict, s[:C, C2:], 0.0) for s in scores]
        a_r = [jnp.concatenate([jnp.where(incl, s[C:, :C2], 0.0), jnp.where(incl, -s[C:, C2:], 0.0)],
                               axis=1).astype(BF16) for s in scores]
        z = [_dot(a_kk_k[g], s_v[g]) + proj[g][:C] for g in pairs]
        tick()

        pw = m_b
        sign = -1.0
        n = 1
        while n < C:
            last = 2 * n >= C
            nxt = []
            for g in pairs:
                if last:
                    pz = _dot(pw[g].astype(BF16), stack(z[g]))
                else:
                    both = _dot(pw[g].astype(BF16), jnp.concatenate([block_diag(pw[g]), stack(z[g])], axis=1))
                    nxt.append(both[:, :C2])
                    pz = both[:, C2:]
                z[g] = z[g] + sign * pz
            tick()
            pw = nxt
            sign = 1.0
            n *= 2

        v_sa = [jnp.concatenate([s_v[g], stack(z[g])], axis=0) for g in pairs]
        ys = []
        for i in pairs:
            s, g = chains[i]
            ys.append(proj[i][C:] + _dot(a_r[i], v_sa[i]))
            state_ref[s, g] = s_prev[i] * lanes("w_c", i) + _dot_tn(v_sa[i], kb_c[i])
        return {s: jnp.concatenate(ys[n * N_PAIRS:(n + 1) * N_PAIRS], axis=1) for n, s in enumerate(streams)}

    first = list(range(max(1, n_streams // 2)))
    second = list(range(len(first), n_streams))
    for s in first:
        for _ in prepare(s):
            pass
    pending = [prepare(s) for s in second]

    def tick():
        for gen in pending:
            next(gen, None)

    y_all = advance(first, tick)
    for gen in pending:
        for _ in gen:
            pass
    if second:
        y_all.update(advance(second, lambda: None))

    for s in range(n_streams):
        y = y_all[s]
        mean = head_sum(y) * (1.0 / HEAD_SIZE)
        d = y - mean
        var = head_sum(d * d) * (1.0 / HEAD_SIZE)
        yn = d * lax.rsqrt(var + GN_EPS) * lnx_w + lnx_b + pre[s]["bonus"]
        o_ref[0, s] = (yn * pre[s]["gate"]).astype(BF16)

    @pl.when(c == pl.num_programs(1) - 1)
    def _():
        sout_ref[0] = state_ref[...]


WKV_STREAMS = 4


def _wkv(proj, shift0, s0_pairs, mu, vecs, w_lowrank, *, layer, nb, t, chunk):
    ns = WKV_STREAMS
    nc = t // chunk
    kern = functools.partial(_wkv_kernel, chunk=chunk)
    grouped = lambda a: a.reshape((nb // ns, ns) + a.shape[1:])
    yg, s_new = pl.pallas_call(
        kern,
        grid=(nb // ns, nc),
        in_specs=[
            pl.BlockSpec((1, ns, chunk, RWKV_COLS), lambda b, c: (b, 0, c, 0)),
            pl.BlockSpec((1, ns, 1, RWKV_COLS), lambda b, c: (b, 0, 0, 0)),
            pl.BlockSpec((1, ns, N_PAIRS, LANES, LANES), lambda b, c: (b, 0, 0, 0, 0)),
            pl.BlockSpec((None, 1, RWKV_COLS), lambda b, c: (layer, 0, 0)),
            pl.BlockSpec((None, 8, RWKV_WIDTH), lambda b, c: (layer, 0, 0)),
            pl.BlockSpec((None, LOWRANK_PAD, 3 * RWKV_WIDTH), lambda b, c: (layer, 0, 0)),
        ],
        out_specs=[
            pl.BlockSpec((1, ns, chunk, RWKV_WIDTH), lambda b, c: (b, 0, c, 0)),
            pl.BlockSpec((1, ns, N_PAIRS, LANES, LANES), lambda b, c: (b, 0, 0, 0, 0)),
        ],
        out_shape=[
            jax.ShapeDtypeStruct((nb // ns, ns, t, RWKV_WIDTH), BF16),
            jax.ShapeDtypeStruct((nb // ns, ns, N_PAIRS, LANES, LANES), F32),
        ],
        scratch_shapes=[pltpu.VMEM((ns, N_PAIRS, LANES, LANES), F32), pltpu.VMEM((ns, 1, RWKV_COLS), F32)],
        compiler_params=pltpu.CompilerParams(
            dimension_semantics=("arbitrary", "arbitrary"), vmem_limit_bytes=VMEM_LIMIT),
        name="wkv",
    )(proj.reshape(nb // ns, ns, t, PROJ_WIDTH), grouped(shift0), grouped(s0_pairs), mu, vecs, w_lowrank)
    return yg.reshape(nb * t, RWKV_WIDTH), s_new.reshape(nb, N_PAIRS, LANES, LANES)


def _merge_kernel(x_ref, m_ref, yg_ref, pp_ref, buf_ref, gt_ref, wo_ref, wp_ref, ps_ref, wout_ref, o_ref, carry_ref,
                  *, pos0, tm):
    i = pl.program_id(1)

    @pl.when(i == 0)
    def _():
        carry_ref[...] = buf_ref[0]

    xp = pp_ref[...]
    full = jnp.concatenate([carry_ref[...], xp], axis=0)
    carry_ref[...] = full[tm:tm + POOL_CARRY]

    pos = pos0 + i * tm + lax.broadcasted_iota(jnp.int32, (tm, POOL_GROUP), 0)
    o_b = []
    for gi, win in enumerate(POOL_WINDOWS):
        cols = slice(gi * POOL_GROUP, (gi + 1) * POOL_GROUP)
        s = full[:, cols]
        width = 1
        while width < win:
            s = s[width:] + s[:-width]
            width *= 2
        s = s[POOL_CARRY - (win - 1):]
        cnt = jnp.minimum(win, pos + 1).astype(F32)
        pooled = s / cnt - xp[:, cols]
        o_b.append(_dot(pooled.astype(BF16), wp_ref[gi]))
    o_b = jnp.concatenate(o_b, axis=1) * ps_ref[...]
    o_a = _dot(yg_ref[...], wo_ref[...])
    gates = jax.nn.sigmoid(gt_ref[...])
    merged = gates[:, :D_MODEL] * o_a + gates[:, D_MODEL:] * o_b
    o_ref[...] = x_ref[...] + m_ref[...][:, 5, :] * _dot(merged.astype(BF16), wout_ref[...])


def _merge(x, mod, yg, proj, pool0, w_o, w_pool, pool_scale, w_out, *, layer, nb, t, b_off, tm, pos0):
    n = x.shape[0]
    nt = t // tm
    kern = functools.partial(_merge_kernel, pos0=pos0, tm=tm)
    row = lambda b, i: (b * nt + i, 0)
    return pl.pallas_call(
        kern,
        grid=(nb, nt),
        in_specs=[
            pl.BlockSpec((tm, D_MODEL), row),
            pl.BlockSpec((None, 1, N_MOD, D_MODEL), lambda b, i: (layer, b_off + b, 0, 0)),
            pl.BlockSpec((tm, RWKV_WIDTH), row),
            pl.BlockSpec((tm, POOL_WIDTH), lambda b, i: (b * nt + i, POOL_COL // POOL_WIDTH)),
            pl.BlockSpec((1, POOL_CARRY, POOL_WIDTH), lambda b, i: (b, 0, 0)),
            pl.BlockSpec((tm, 2 * D_MODEL), lambda b, i: (b * nt + i, GATE_COL // (2 * D_MODEL))),
            pl.BlockSpec((None, RWKV_WIDTH, D_MODEL), lambda b, i: (layer, 0, 0), pipeline_mode=pl.Buffered(1)),
            pl.BlockSpec((None, len(POOL_WINDOWS), POOL_GROUP, POOL_OUT_GROUP), lambda b, i: (layer, 0, 0, 0),
                         pipeline_mode=pl.Buffered(1)),
            pl.BlockSpec((None, 1, D_MODEL), lambda b, i: (layer, 0, 0)),
            pl.BlockSpec((None, D_MODEL, D_MODEL), lambda b, i: (layer, 0, 0), pipeline_mode=pl.Buffered(1)),
        ],
        out_specs=pl.BlockSpec((tm, D_MODEL), row),
        out_shape=jax.ShapeDtypeStruct((n, D_MODEL), F32),
        scratch_shapes=[pltpu.VMEM((POOL_CARRY, POOL_WIDTH), F32)],
        compiler_params=pltpu.CompilerParams(
            dimension_semantics=("arbitrary", "arbitrary"), vmem_limit_bytes=WIDE_VMEM_LIMIT),
        name="merge",
    )(x, mod, yg, proj, pool0, proj, w_o, w_pool, pool_scale, w_out)


def _to_pairs(s):
    b = s.shape[0]
    s = s.reshape(b, N_PAIRS, 2, HEAD_SIZE, HEAD_SIZE)
    z = jnp.zeros_like(s[:, :, 0])
    top = jnp.concatenate([s[:, :, 0], z], axis=-1)
    bot = jnp.concatenate([z, s[:, :, 1]], axis=-1)
    return jnp.concatenate([top, bot], axis=-2)


def _from_pairs(sp):
    b = sp.shape[0]
    h0 = sp[:, :, :HEAD_SIZE, :HEAD_SIZE]
    h1 = sp[:, :, HEAD_SIZE:, HEAD_SIZE:]
    return jnp.stack([h0, h1], axis=2).reshape(b, N_HEADS, HEAD_SIZE, HEAD_SIZE)


def _stream_set(x, t, nb, b_off, *, ffn_tm, proj_tm, merge_tm, chunk, pos0):
    return dict(x=x.reshape(nb * t, D_MODEL), t=t, nb=nb, b_off=b_off, ffn_tm=ffn_tm, proj_tm=proj_tm,
                merge_tm=merge_tm, chunk=chunk, pos0=pos0)


def kernel(x_prompt, x_sample, c_prompt, c_sample, state_wkv, state_shift, state_pool, norm_g, w_mod, b_mod,
           ffn_w1, ffn_w3, ffn_w2, w_in, shift_mu, w0, w_decay, a0, w_a, w_gate, k_k, k_a, r_k, lnx_w, lnx_b,
           w_o_rwkv, w_pool, pool_scale, w_out, final_g):
    bp, seq, _ = x_prompt.shape
    bs, dseq, _ = x_sample.shape
    dt = x_prompt.dtype

    pad = jnp.zeros((DEPTH, D_MODEL, GATE_COL - SHIFT_WIDTH), BF16)
    w_in_b = w_in.astype(BF16)
    w_in_p = jnp.concatenate([w_in_b[:, :, :SHIFT_WIDTH], pad, w_in_b[:, :, SHIFT_WIDTH + POOL_WIDTH:],
                              w_in_b[:, :, SHIFT_WIDTH:SHIFT_WIDTH + POOL_WIDTH]], axis=2)
    w_o_b = w_o_rwkv.astype(BF16)
    w_pool_b = w_pool.astype(BF16)
    w_out_b = w_out.astype(BF16)
    zr = lambda rows: jnp.zeros((DEPTH, rows, RWKV_WIDTH), F32)
    w_lr = jnp.concatenate([
        jnp.concatenate([w_decay, zr(LOWRANK_PAD - DECAY_RANK)], axis=1),
        jnp.concatenate([zr(DECAY_RANK), w_a, zr(LOWRANK_PAD - DECAY_RANK - A_RANK)], axis=1),
        jnp.concatenate([zr(DECAY_RANK + A_RANK), w_gate, zr(LOWRANK_PAD - LOWRANK)], axis=1),
    ], axis=2).astype(BF16)
    vecs = jnp.stack([w0, a0, k_k, k_a, lnx_w, lnx_b, r_k.reshape(DEPTH, RWKV_WIDTH),
                      jnp.zeros_like(w0)], axis=1)
    mu = jnp.pad(shift_mu, ((0, 0), (0, RWKV_COLS - SHIFT_WIDTH))).reshape(DEPTH, 1, RWKV_COLS)
    norm_g4 = norm_g.reshape(DEPTH, 3, 1, D_MODEL)
    fg = final_g.reshape(1, D_MODEL)
    ps = pool_scale.reshape(DEPTH, 1, D_MODEL)

    mod = _modulation(jnp.concatenate([c_prompt, c_sample], axis=0), w_mod, b_mod)
    mod = mod.reshape(DEPTH, bp + bs, N_MOD, D_MODEL)

    prompt = _stream_set(x_prompt, seq, bp, 0, ffn_tm=min(FFN_ROWS, seq), proj_tm=min(PROJ_ROWS, seq),
                         merge_tm=min(MERGE_ROWS, seq), chunk=min(WKV_CHUNK, seq), pos0=0)
    sample = _stream_set(x_sample, dseq, bs, bp, ffn_tm=bs * dseq, proj_tm=bs * dseq, merge_tm=dseq,
                         chunk=dseq, pos0=PAST_LEN)

    outs = {"p": ([], [], []), "s": ([], [], [])}
    ffn_bf16 = {}
    for l in range(DEPTH):
        for tag, st in (("s", sample), ("p", prompt)):
            nb, t = st["nb"], st["t"]
            if tag == "p":
                shift0 = jnp.zeros((nb, 1, RWKV_COLS), F32)
                s0 = jnp.zeros((nb, N_PAIRS, LANES, LANES), F32)
                pool0 = jnp.zeros((nb, POOL_CARRY, POOL_WIDTH), F32)
            else:
                shift0 = jnp.pad(state_shift[l], ((0, 0), (0, RWKV_COLS - SHIFT_WIDTH))).reshape(nb, 1, RWKV_COLS)
                s0 = _to_pairs(state_wkv[l].astype(F32))
                pool0 = jnp.pad(state_pool[l], ((0, 0), (POOL_CARRY - POOL_BUF, 0), (0, 0)))
            common = dict(layer=l, t=t, b_off=st["b_off"])

            def ffn(x, which, sub, final):
                args = dict(which=which, sub=sub, tm=st["ffn_tm"], final=final, **common)
                if tag == "s":
                    x, *ffn_bf16[l, which] = _ffn(x, mod, norm_g4, ffn_w1, ffn_w3, ffn_w2, fg, emit_bf16=True, **args)
                    return x
                return _ffn(x, mod, norm_g4, *ffn_bf16[l, which], fg, **args)

            x = st["x"]
            x = ffn(x, 0, 0, False)
            proj = _in_proj(x, mod, norm_g4, w_in_p, tm=st["proj_tm"], **common)
            yg, s_new = _wkv(proj, shift0, s0, mu, vecs, w_lr, layer=l, nb=nb, t=t, chunk=st["chunk"])
            x = _merge(x, mod, yg, proj, pool0, w_o_b, w_pool_b, ps, w_out_b, nb=nb,
                       tm=st["merge_tm"], pos0=st["pos0"], **common)
            x = ffn(x, 1, 2, l == DEPTH - 1)
            st["x"] = x
            wkv_l, shift_l, pool_l = outs[tag]
            wkv_l.append(_from_pairs(s_new).astype(dt))
            proj3 = proj.reshape(nb, t, PROJ_WIDTH)
            shift_l.append(proj3[:, -1, :SHIFT_WIDTH])
            pool_l.append(proj3[:, t - POOL_BUF:, POOL_COL:])

    yp = prompt["x"].reshape(bp, seq, D_MODEL)
    ys = sample["x"].reshape(bs, dseq, D_MODEL)
    pw, psh, ppl = (jnp.stack(v) for v in outs["p"])
    sw, ssh, spl = (jnp.stack(v) for v in outs["s"])
    return (yp, ys, pw, psh, ppl, sw, ssh, spl)
```

```python
import functools

import jax
import jax.numpy as jnp
from jax import lax
from jax.experimental import pallas as pl
from jax.experimental.pallas import tpu as pltpu

F32 = jnp.float32
BF16 = jnp.bfloat16

D_MODEL = 2048
DEPTH = 4
RWKV_WIDTH = D_MODEL // 2
HEAD_SIZE = 64
N_HEADS = RWKV_WIDTH // HEAD_SIZE
DECAY_RANK = 64
A_RANK = 64
GATE_RANK = 160
POOL_WIDTH = D_MODEL // 2
POOL_WINDOWS = (2, 4, 8, 16)
POOL_GROUP = POOL_WIDTH // len(POOL_WINDOWS)
POOL_OUT_GROUP = D_MODEL // len(POOL_WINDOWS)
POOL_BUF = max(POOL_WINDOWS) - 1
D_FF = ((8 * D_MODEL // 3 + 255) // 256) * 256
SHIFT_WIDTH = 3 * RWKV_WIDTH + DECAY_RANK + A_RANK + GATE_RANK
N_MOD = 9
RMS_EPS = 1e-6
GN_EPS = 64e-5

LANES = 128
N_PAIRS = RWKV_WIDTH // LANES
LOWRANK = DECAY_RANK + A_RANK + GATE_RANK
LOWRANK_PAD = 3 * LANES
RWKV_COLS = 3 * RWKV_WIDTH + LOWRANK_PAD
PROJ_TILE = 2304
GATE_COL = 2 * D_MODEL
POOL_COL = 4 * D_MODEL
PROJ_WIDTH = POOL_COL + POOL_WIDTH
POOL_CARRY = 16
PAST_LEN = 1024

FFN_ROWS = 1024
PROJ_ROWS = 1024
MERGE_ROWS = 512
WKV_CHUNK = 64
V7X_VMEM_BYTES = 64 * 1024 * 1024
VMEM_LIMIT = V7X_VMEM_BYTES - 8 * 1024 * 1024
WIDE_VMEM_LIMIT = V7X_VMEM_BYTES - 2 * 1024 * 1024


def _dot(a, b):
    return jnp.dot(a, b, preferred_element_type=F32)


def _dot_nt(a, b):
    return lax.dot_general(a, b, (((1,), (1,)), ((), ())), preferred_element_type=F32)


def _dot_tn(a, b):
    return lax.dot_general(a, b, (((0,), (0,)), ((), ())), preferred_element_type=F32)


def _rows(v, tm):
    g, d = v.shape
    if g == 1:
        return v
    return jnp.broadcast_to(v[:, None, :], (g, tm // g, d)).reshape(tm, d)


NORM_ROWS = 16


ROW_SPLIT = 4


def _modulated_norm_blocks(h_ref, x_ref, g_ref, m_ref, sub):
    tm = x_ref.shape[0]
    rows_per_stream = tm // m_ref.shape[0]
    block = tm // ROW_SPLIT
    gain = g_ref[...]
    cache = {}

    def scale_shift(s):
        if s not in cache:
            cache[s] = (gain * (1.0 + m_ref[s, 3 * sub + 1:3 * sub + 2, :]), m_ref[s, 3 * sub:3 * sub + 1, :])
        return cache[s]

    def store(r):
        for r0 in range(r * block, (r + 1) * block, NORM_ROWS):
            gs, shift = scale_shift(r0 // rows_per_stream)
            x = x_ref[r0:r0 + NORM_ROWS, :]
            y = x * lax.rsqrt(jnp.mean(x * x, axis=-1, keepdims=True) + RMS_EPS)
            h_ref[r0:r0 + NORM_ROWS, :] = (y * gs + shift).astype(BF16)

    return store, block


def _mod_kernel(c_ref, w_ref, b_ref, o_ref):
    c = c_ref[...]
    s = (c * jax.nn.sigmoid(c)).astype(BF16)
    o_ref[0] = _dot(s, w_ref[0].astype(BF16)) + b_ref[0]


def _modulation(c_all, w_mod, b_mod):
    nb = c_all.shape[0]
    width = N_MOD * D_MODEL
    tn = 1024
    return pl.pallas_call(
        _mod_kernel,
        grid=(DEPTH, width // tn),
        in_specs=[
            pl.BlockSpec((nb, D_MODEL), lambda l, j: (0, 0)),
            pl.BlockSpec((1, D_MODEL, tn), lambda l, j: (l, 0, j)),
            pl.BlockSpec((1, 1, tn), lambda l, j: (l, 0, j)),
        ],
        out_specs=pl.BlockSpec((1, nb, tn), lambda l, j: (l, 0, j)),
        out_shape=jax.ShapeDtypeStruct((DEPTH, nb, width), F32),
        compiler_params=pltpu.CompilerParams(
            dimension_semantics=("arbitrary", "arbitrary"), vmem_limit_bytes=VMEM_LIMIT),
        name="modulation",
    )(c_all, w_mod, b_mod.reshape(DEPTH, 1, width))


def _mod_spec(layer, tm, t, b_off):
    g = max(1, tm // t)
    tiles_per_stream = max(1, t // tm)
    assert b_off % g == 0 and (tm % t == 0 or t % tm == 0)
    base = b_off // g
    return pl.BlockSpec((None, g, N_MOD, D_MODEL),
                        lambda i, j: (layer, base + i // tiles_per_stream, 0, 0))


FF_TILE = 512


def _ffn_kernel(x_ref, m_ref, g_ref, w1_ref, w3_ref, w2_ref, fg_ref, o_ref, *rest, sub, final, emit_bf16):
    if emit_bf16:
        w1b_ref, w3b_ref, w2b_ref, h_ref = rest
        w1b_ref[...] = w1_ref[...].astype(BF16)
        w3b_ref[...] = w3_ref[...].astype(BF16)
        w2b_ref[...] = w2_ref[...].astype(BF16)
        w1_ref, w3_ref, w2_ref = w1b_ref, w3b_ref, w2b_ref
    else:
        (h_ref,) = rest
    j = pl.program_id(1)
    last = pl.num_programs(1) - 1
    tm = x_ref.shape[0]

    def gated(h):
        half = FF_TILE // 2
        parts = []
        for c0 in (0, half):
            a = _dot(h, w1_ref[:, c0:c0 + half])
            parts.append((a * jax.nn.sigmoid(a) * _dot(h, w3_ref[:, c0:c0 + half])).astype(BF16))
        return jnp.concatenate(parts, axis=1)

    @pl.when(j == 0)
    def _():
        store_norm, block = _modulated_norm_blocks(h_ref, x_ref, g_ref, m_ref, sub)
        store_norm(0)
        for r in range(ROW_SPLIT):
            if r + 1 < ROW_SPLIT:
                store_norm(r + 1)
            rows = slice(r * block, (r + 1) * block)
            o_ref[rows, :] = _dot(gated(h_ref[rows, :]), w2_ref[...])

    @pl.when(jnp.logical_and(j > 0, j < last))
    def _():
        o_ref[...] += _dot(gated(h_ref[...]), w2_ref[...])

    @pl.when(j == last)
    def _():
        block = tm // ROW_SPLIT
        u = gated(h_ref[...])
        gate = 0.5 * _rows(m_ref[...][:, 3 * sub + 2, :], tm)
        for r in range(ROW_SPLIT):
            rows = slice(r * block, (r + 1) * block)
            acc = o_ref[rows, :] + _dot(u[rows, :], w2_ref[...])
            y = x_ref[rows, :] + (gate if gate.shape[0] == 1 else gate[rows, :]) * acc
            if final:
                y = y * lax.rsqrt(jnp.mean(y * y, axis=-1, keepdims=True) + RMS_EPS) * fg_ref[...]
            o_ref[rows, :] = y


def _ffn(x, mod, norm_g, w1, w3, w2, final_g, *, layer, which, sub, t, b_off, tm, final, emit_bf16=False):
    n = x.shape[0]
    tf = FF_TILE
    assert D_FF // tf >= 2
    kern = functools.partial(_ffn_kernel, sub=sub, final=final, emit_bf16=emit_bf16)
    w13_tile = pl.BlockSpec((D_MODEL, tf), lambda i, j: (0, j))
    w2_tile = pl.BlockSpec((tf, D_MODEL), lambda i, j: (j, 0))
    out_specs = pl.BlockSpec((tm, D_MODEL), lambda i, j: (i, 0))
    out_shape = jax.ShapeDtypeStruct((n, D_MODEL), F32)
    if emit_bf16:
        assert n == tm
        w_specs = [
            pl.BlockSpec((None, None, D_MODEL, tf), lambda i, j: (layer, which, 0, j)),
            pl.BlockSpec((None, None, D_MODEL, tf), lambda i, j: (layer, which, 0, j)),
            pl.BlockSpec((None, None, tf, D_MODEL), lambda i, j: (layer, which, j, 0)),
        ]
        out_specs = [out_specs, w13_tile, w13_tile, w2_tile]
        out_shape = [out_shape, jax.ShapeDtypeStruct((D_MODEL, D_FF), BF16),
                     jax.ShapeDtypeStruct((D_MODEL, D_FF), BF16), jax.ShapeDtypeStruct((D_FF, D_MODEL), BF16)]
    else:
        w_specs = [w13_tile, w13_tile, w2_tile]
    return pl.pallas_call(
        kern,
        grid=(n // tm, D_FF // tf),
        in_specs=[
            pl.BlockSpec((tm, D_MODEL), lambda i, j: (i, 0)),
            _mod_spec(layer, tm, t, b_off),
            pl.BlockSpec((None, None, 1, D_MODEL), lambda i, j: (layer, sub, 0, 0)),
            *w_specs,
            pl.BlockSpec((1, D_MODEL), lambda i, j: (0, 0)),
        ],
        out_specs=out_specs,
        out_shape=out_shape,
        scratch_shapes=[pltpu.VMEM((tm, D_MODEL), BF16)],
        compiler_params=pltpu.CompilerParams(
            dimension_semantics=("arbitrary", "arbitrary"), vmem_limit_bytes=WIDE_VMEM_LIMIT),
        name="ffn",
    )(x, mod, norm_g, w1, w3, w2, final_g)


def _proj_kernel(x_ref, m_ref, g_ref, w_ref, o_ref, h_ref):
    j = pl.program_id(1)

    @pl.when(j == 0)
    def _():
        store_norm, block = _modulated_norm_blocks(h_ref, x_ref, g_ref, m_ref, 1)
        store_norm(0)
        for r in range(ROW_SPLIT):
            if r + 1 < ROW_SPLIT:
                store_norm(r + 1)
            rows = slice(r * block, (r + 1) * block)
            o_ref[rows, :] = _dot(h_ref[rows, :], w_ref[...])

    @pl.when(j > 0)
    def _():
        o_ref[...] = _dot(h_ref[...], w_ref[...])


def _in_proj(x, mod, norm_g, w_in, *, layer, t, b_off, tm):
    n = x.shape[0]
    return pl.pallas_call(
        _proj_kernel,
        grid=(n // tm, PROJ_WIDTH // PROJ_TILE),
        in_specs=[
            pl.BlockSpec((tm, D_MODEL), lambda i, j: (i, 0)),
            _mod_spec(layer, tm, t, b_off),
            pl.BlockSpec((None, None, 1, D_MODEL), lambda i, j: (layer, 1, 0, 0)),
            pl.BlockSpec((None, D_MODEL, PROJ_TILE), lambda i, j: (layer, 0, j)),
        ],
        out_specs=pl.BlockSpec((tm, PROJ_TILE), lambda i, j: (i, j)),
        out_shape=jax.ShapeDtypeStruct((n, PROJ_WIDTH), F32),
        scratch_shapes=[pltpu.VMEM((tm, D_MODEL), BF16)],
        compiler_params=pltpu.CompilerParams(
            dimension_semantics=("arbitrary", "arbitrary"), vmem_limit_bytes=WIDE_VMEM_LIMIT),
        name="in_proj",
    )(x, mod, norm_g, w_in)


def _split3(x):
    hi = x.astype(BF16)
    r1 = x - hi.astype(F32)
    mid = r1.astype(BF16)
    lo = (r1 - mid.astype(F32)).astype(BF16)
    return hi, mid, lo


def _wkv_kernel(p_ref, sh0_ref, s0_ref, mu_ref, vec_ref, wlr_ref, o_ref, sout_ref, state_ref, prev_ref, *, chunk):
    c = pl.program_id(1)
    C = chunk
    C2 = 2 * C
    n_streams = p_ref.shape[1]

    @pl.when(c == 0)
    def _():
        state_ref[...] = s0_ref[0]
        prev_ref[...] = sh0_ref[0]

    w0, a0, k_k, k_a, lnx_w, lnx_b, r_k = (vec_ref[i:i + 1, :] for i in range(7))
    lane = lax.broadcasted_iota(jnp.int32, (1, LANES), 1)
    head0 = lane < HEAD_SIZE
    ri = lax.broadcasted_iota(jnp.int32, (LANES, LANES), 0)
    ci = lax.broadcasted_iota(jnp.int32, (LANES, LANES), 1)
    head_ones = ((ri < HEAD_SIZE) == (ci < HEAD_SIZE)).astype(BF16)
    tr = lax.broadcasted_iota(jnp.int32, (C, C), 0)
    tc = lax.broadcasted_iota(jnp.int32, (C, C), 1)
    cum_ones = (tc <= tr).astype(BF16)
    row = lax.broadcasted_iota(jnp.int32, (C, 1), 0)

    def head_sum(x):
        st = jnp.concatenate([x[:, g * LANES:(g + 1) * LANES] for g in range(N_PAIRS)], axis=0)
        s = _dot(st.astype(BF16), head_ones)
        return jnp.concatenate([s[g * C:(g + 1) * C] for g in range(N_PAIRS)], axis=1)

    def prepare(s):
        p = p_ref[0, s]
        prev = jnp.where(row == 0, prev_ref[s], pltpu.roll(p, 1, 0))
        prev_ref[s] = p[C - 1:C, :]
        xs = p + (prev - p) * mu_ref[...]
        yield

        r = xs[:, 0:RWKV_WIDTH]
        k = xs[:, RWKV_WIDTH:2 * RWKV_WIDTH]
        v = xs[:, 2 * RWKV_WIDTH:3 * RWKV_WIDTH]
        z = xs[:, 3 * RWKV_WIDTH:RWKV_COLS]
        zc = lax.broadcasted_iota(jnp.int32, z.shape, 1)
        zact = jnp.where(zc < DECAY_RANK, jnp.tanh(z),
                         jnp.where(zc < DECAY_RANK + A_RANK, z, jax.nn.sigmoid(z)))
        zact = zact.astype(BF16)
        n_da = DECAY_RANK + A_RANK
        lr = _dot(zact[:, :n_da], wlr_ref[:n_da, :2 * RWKV_WIDTH])
        gate = _dot(zact[:, n_da:], wlr_ref[n_da:, 2 * RWKV_WIDTH:])
        yield

        zz = -(w0 + lr[:, 0:RWKV_WIDTH])
        softplus = jnp.maximum(zz, 0.0) + jnp.log(1.0 + jnp.exp(-jnp.abs(zz)))
        lw = -jnp.exp(-softplus - 0.5)
        a = jax.nn.sigmoid(a0 + lr[:, RWKV_WIDTH:2 * RWKV_WIDTH])
        yield

        hi, mid, lo = _split3(lw)
        cs = _dot(cum_ones, hi) + _dot(cum_ones, mid) + _dot(cum_ones, lo)
        yield

        w_in = jnp.exp(cs)
        w_inv = jnp.exp(-cs)
        w_ex = jnp.exp(cs - lw)
        yield

        kk = k * k_k
        k2 = k * (1.0 + (a - 1.0) * k_a)
        kk = kk * lax.rsqrt(jnp.maximum(head_sum(kk * kk), 1e-24))
        yield

        b = kk * a
        k_h = k2 * w_inv
        b_h = b * w_inv
        w_c = w_in[C - 1:C, :]
        yield

        pre[s] = dict(v=v, r_t=r * w_in, kk_t=kk * w_ex, k_h=k_h, b_h=b_h, w_c=w_c, k_b=k_h * w_c, b_b=b_h * w_c,
                      gate=gate, bonus=head_sum(r * k2 * r_k) * v)

    pre = {}

    tr2 = lax.broadcasted_iota(jnp.int32, (C, C2), 0)
    tc2 = lax.broadcasted_iota(jnp.int32, (C, C2), 1)
    col_head0 = tc2 < C
    ts2 = jnp.where(col_head0, tc2, tc2 - C)
    strict = ts2 < tr2
    incl = ts2 <= tr2

    def stack(x):
        xb = x.astype(BF16)
        zero = jnp.zeros_like(xb)
        return jnp.concatenate([jnp.where(head0, xb, zero), jnp.where(head0, zero, xb)], axis=0)

    def block_diag(p):
        pb = p.astype(BF16)
        zero = jnp.zeros_like(pb)
        return jnp.concatenate([jnp.where(col_head0, pb, zero), jnp.where(col_head0, zero, pb)], axis=0)

    def advance(streams, tick):
        chains = [(s, g) for s in streams for g in range(N_PAIRS)]
        pairs = range(len(chains))

        def lanes(name, i):
            s, g = chains[i]
            return pre[s][name][:, g * LANES:(g + 1) * LANES]

        s_prev = [state_ref[s, g] for s, g in chains]
        q = [jnp.concatenate([lanes("kk_t", i), lanes("r_t", i)], axis=0).astype(BF16) for i in pairs]
        kb = [jnp.concatenate([stack(lanes("k_h", i)), stack(lanes("b_h", i))], axis=0) for i in pairs]
        s_v = [stack(lanes("v", i)) for i in pairs]
        kb_c = [jnp.concatenate([stack(lanes("k_b", i)), -stack(lanes("b_b", i))], axis=0) for i in pairs]

        scores = [_dot_nt(q[g], kb[g]) for g in pairs]
        proj = [_dot_nt(q[g], s_prev[g].astype(BF16)) for g in pairs]
        tick()
        a_kk_k = [jnp.where(strict, s[:C, :C2], 0.0).astype(BF16) for s in scores]
        m_b = [jnp.where(strict, s[:C, C2:], 0.0) for s in scores]
        a_r = [jnp.concatenate([jnp.where(incl, s[C:, :C2], 0.0), jnp.where(incl, -s[C:, C2:], 0.0)],
                               axis=1).astype(BF16) for s in scores]
        z = [_dot(a_kk_k[g], s_v[g]) + proj[g][:C] for g in pairs]
        tick()

        pw = m_b
        sign = -1.0
        n = 1
        while n < C:
            last = 2 * n >= C
            nxt = []
            for g in pairs:
                if last:
                    pz = _dot(pw[g].astype(BF16), stack(z[g]))
                else:
                    both = _dot(pw[g].astype(BF16), jnp.concatenate([block_diag(pw[g]), stack(z[g])], axis=1))
                    nxt.append(both[:, :C2])
                    pz = both[:, C2:]
                z[g] = z[g] + sign * pz
            tick()
            pw = nxt
            sign = 1.0
            n *= 2

        v_sa = [jnp.concatenate([s_v[g], stack(z[g])], axis=0) for g in pairs]
        ys = []
        for i in pairs:
            s, g = chains[i]
            ys.append(proj[i][C:] + _dot(a_r[i], v_sa[i]))
            state_ref[s, g] = s_prev[i] * lanes("w_c", i) + _dot_tn(v_sa[i], kb_c[i])
        return {s: jnp.concatenate(ys[n * N_PAIRS:(n + 1) * N_PAIRS], axis=1) for n, s in enumerate(streams)}

    first = list(range(max(1, n_streams // 2)))
    second = list(range(len(first), n_streams))
    for s in first:
        for _ in prepare(s):
            pass
    pending = [prepare(s) for s in second]

    def tick():
        for gen in pending:
            next(gen, None)

    y_all = advance(first, tick)
    for gen in pending:
        for _ in gen:
            pass
    if second:
        y_all.update(advance(second, lambda: None))

    for s in range(n_streams):
        y = y_all[s]
        mean = head_sum(y) * (1.0 / HEAD_SIZE)
        d = y - mean
        var = head_sum(d * d) * (1.0 / HEAD_SIZE)
        yn = d * lax.rsqrt(var + GN_EPS) * lnx_w + lnx_b + pre[s]["bonus"]
        o_ref[0, s] = (yn * pre[s]["gate"]).astype(BF16)

    @pl.when(c == pl.num_programs(1) - 1)
    def _():
        sout_ref[0] = state_ref[...]


WKV_STREAMS = 4


def _wkv(proj, shift0, s0_pairs, mu, vecs, w_lowrank, *, layer, nb, t, chunk):
    ns = WKV_STREAMS
    nc = t // chunk
    kern = functools.partial(_wkv_kernel, chunk=chunk)
    grouped = lambda a: a.reshape((nb // ns, ns) + a.shape[1:])
    yg, s_new = pl.pallas_call(
        kern,
        grid=(nb // ns, nc),
        in_specs=[
            pl.BlockSpec((1, ns, chunk, RWKV_COLS), lambda b, c: (b, 0, c, 0)),
            pl.BlockSpec((1, ns, 1, RWKV_COLS), lambda b, c: (b, 0, 0, 0)),
            pl.BlockSpec((1, ns, N_PAIRS, LANES, LANES), lambda b, c: (b, 0, 0, 0, 0)),
            pl.BlockSpec((None, 1, RWKV_COLS), lambda b, c: (layer, 0, 0)),
            pl.BlockSpec((None, 8, RWKV_WIDTH), lambda b, c: (layer, 0, 0)),
            pl.BlockSpec((None, LOWRANK_PAD, 3 * RWKV_WIDTH), lambda b, c: (layer, 0, 0)),
        ],
        out_specs=[
            pl.BlockSpec((1, ns, chunk, RWKV_WIDTH), lambda b, c: (b, 0, c, 0)),
            pl.BlockSpec((1, ns, N_PAIRS, LANES, LANES), lambda b, c: (b, 0, 0, 0, 0)),
        ],
        out_shape=[
            jax.ShapeDtypeStruct((nb // ns, ns, t, RWKV_WIDTH), BF16),
            jax.ShapeDtypeStruct((nb // ns, ns, N_PAIRS, LANES, LANES), F32),
        ],
        scratch_shapes=[pltpu.VMEM((ns, N_PAIRS, LANES, LANES), F32), pltpu.VMEM((ns, 1, RWKV_COLS), F32)],
        compiler_params=pltpu.CompilerParams(
            dimension_semantics=("arbitrary", "arbitrary"), vmem_limit_bytes=VMEM_LIMIT),
        name="wkv",
    )(proj.reshape(nb // ns, ns, t, PROJ_WIDTH), grouped(shift0), grouped(s0_pairs), mu, vecs, w_lowrank)
    return yg.reshape(nb * t, RWKV_WIDTH), s_new.reshape(nb, N_PAIRS, LANES, LANES)


def _merge_kernel(x_ref, m_ref, yg_ref, pp_ref, buf_ref, gt_ref, wo_ref, wp_ref, ps_ref, wout_ref, o_ref, carry_ref,
                  *, pos0, tm):
    i = pl.program_id(1)

    @pl.when(i == 0)
    def _():
        carry_ref[...] = buf_ref[0]

    xp = pp_ref[...]
    full = jnp.concatenate([carry_ref[...], xp], axis=0)
    carry_ref[...] = full[tm:tm + POOL_CARRY]

    pos = pos0 + i * tm + lax.broadcasted_iota(jnp.int32, (tm, POOL_GROUP), 0)
    o_b = []
    for gi, win in enumerate(POOL_WINDOWS):
        cols = slice(gi * POOL_GROUP, (gi + 1) * POOL_GROUP)
        s = full[:, cols]
        width = 1
        while width < win:
            s = s[width:] + s[:-width]
            width *= 2
        s = s[POOL_CARRY - (win - 1):]
        cnt = jnp.minimum(win, pos + 1).astype(F32)
        pooled = s / cnt - xp[:, cols]
        o_b.append(_dot(pooled.astype(BF16), wp_ref[gi]))
    o_b = jnp.concatenate(o_b, axis=1) * ps_ref[...]
    o_a = _dot(yg_ref[...], wo_ref[...])
    gates = jax.nn.sigmoid(gt_ref[...])
    merged = gates[:, :D_MODEL] * o_a + gates[:, D_MODEL:] * o_b
    o_ref[...] = x_ref[...] + m_ref[...][:, 5, :] * _dot(merged.astype(BF16), wout_ref[...])


def _merge(x, mod, yg, proj, pool0, w_o, w_pool, pool_scale, w_out, *, layer, nb, t, b_off, tm, pos0):
    n = x.shape[0]
    nt = t // tm
    kern = functools.partial(_merge_kernel, pos0=pos0, tm=tm)
    row = lambda b, i: (b * nt + i, 0)
    return pl.pallas_call(
        kern,
        grid=(nb, nt),
        in_specs=[
            pl.BlockSpec((tm, D_MODEL), row),
            pl.BlockSpec((None, 1, N_MOD, D_MODEL), lambda b, i: (layer, b_off + b, 0, 0)),
            pl.BlockSpec((tm, RWKV_WIDTH), row),
            pl.BlockSpec((tm, POOL_WIDTH), lambda b, i: (b * nt + i, POOL_COL // POOL_WIDTH)),
            pl.BlockSpec((1, POOL_CARRY, POOL_WIDTH), lambda b, i: (b, 0, 0)),
            pl.BlockSpec((tm, 2 * D_MODEL), lambda b, i: (b * nt + i, GATE_COL // (2 * D_MODEL))),
            pl.BlockSpec((None, RWKV_WIDTH, D_MODEL), lambda b, i: (layer, 0, 0), pipeline_mode=pl.Buffered(1)),
            pl.BlockSpec((None, len(POOL_WINDOWS), POOL_GROUP, POOL_OUT_GROUP), lambda b, i: (layer, 0, 0, 0),
                         pipeline_mode=pl.Buffered(1)),
            pl.BlockSpec((None, 1, D_MODEL), lambda b, i: (layer, 0, 0)),
            pl.BlockSpec((None, D_MODEL, D_MODEL), lambda b, i: (layer, 0, 0), pipeline_mode=pl.Buffered(1)),
        ],
        out_specs=pl.BlockSpec((tm, D_MODEL), row),
        out_shape=jax.ShapeDtypeStruct((n, D_MODEL), F32),
        scratch_shapes=[pltpu.VMEM((POOL_CARRY, POOL_WIDTH), F32)],
        compiler_params=pltpu.CompilerParams(
            dimension_semantics=("arbitrary", "arbitrary"), vmem_limit_bytes=WIDE_VMEM_LIMIT),
        name="merge",
    )(x, mod, yg, proj, pool0, proj, w_o, w_pool, pool_scale, w_out)


def _to_pairs(s):
    b = s.shape[0]
    s = s.reshape(b, N_PAIRS, 2, HEAD_SIZE, HEAD_SIZE)
    z = jnp.zeros_like(s[:, :, 0])
    top = jnp.concatenate([s[:, :, 0], z], axis=-1)
    bot = jnp.concatenate([z, s[:, :, 1]], axis=-1)
    return jnp.concatenate([top, bot], axis=-2)


def _from_pairs(sp):
    b = sp.shape[0]
    h0 = sp[:, :, :HEAD_SIZE, :HEAD_SIZE]
    h1 = sp[:, :, HEAD_SIZE:, HEAD_SIZE:]
    return jnp.stack([h0, h1], axis=2).reshape(b, N_HEADS, HEAD_SIZE, HEAD_SIZE)


def _stream_set(x, t, nb, b_off, *, ffn_tm, proj_tm, merge_tm, chunk, pos0):
    return dict(x=x.reshape(nb * t, D_MODEL), t=t, nb=nb, b_off=b_off, ffn_tm=ffn_tm, proj_tm=proj_tm,
                merge_tm=merge_tm, chunk=chunk, pos0=pos0)


def kernel(x_prompt, x_sample, c_prompt, c_sample, state_wkv, state_shift, state_pool, norm_g, w_mod, b_mod,
           ffn_w1, ffn_w3, ffn_w2, w_in, shift_mu, w0, w_decay, a0, w_a, w_gate, k_k, k_a, r_k, lnx_w, lnx_b,
           w_o_rwkv, w_pool, pool_scale, w_out, final_g):
    bp, seq, _ = x_prompt.shape
    bs, dseq, _ = x_sample.shape
    dt = x_prompt.dtype

    pad = jnp.zeros((DEPTH, D_MODEL, GATE_COL - SHIFT_WIDTH), BF16)
    w_in_b = w_in.astype(BF16)
    w_in_p = jnp.concatenate([w_in_b[:, :, :SHIFT_WIDTH], pad, w_in_b[:, :, SHIFT_WIDTH + POOL_WIDTH:],
                              w_in_b[:, :, SHIFT_WIDTH:SHIFT_WIDTH + POOL_WIDTH]], axis=2)
    w_o_b = w_o_rwkv.astype(BF16)
    w_pool_b = w_pool.astype(BF16)
    w_out_b = w_out.astype(BF16)
    zr = lambda rows: jnp.zeros((DEPTH, rows, RWKV_WIDTH), F32)
    w_lr = jnp.concatenate([
        jnp.concatenate([w_decay, zr(LOWRANK_PAD - DECAY_RANK)], axis=1),
        jnp.concatenate([zr(DECAY_RANK), w_a, zr(LOWRANK_PAD - DECAY_RANK - A_RANK)], axis=1),
        jnp.concatenate([zr(DECAY_RANK + A_RANK), w_gate, zr(LOWRANK_PAD - LOWRANK)], axis=1),
    ], axis=2).astype(BF16)
    vecs = jnp.stack([w0, a0, k_k, k_a, lnx_w, lnx_b, r_k.reshape(DEPTH, RWKV_WIDTH),
                      jnp.zeros_like(w0)], axis=1)
    mu = jnp.pad(shift_mu, ((0, 0), (0, RWKV_COLS - SHIFT_WIDTH))).reshape(DEPTH, 1, RWKV_COLS)
    norm_g4 = norm_g.reshape(DEPTH, 3, 1, D_MODEL)
    fg = final_g.reshape(1, D_MODEL)
    ps = pool_scale.reshape(DEPTH, 1, D_MODEL)

    mod = _modulation(jnp.concatenate([c_prompt, c_sample], axis=0), w_mod, b_mod)
    mod = mod.reshape(DEPTH, bp + bs, N_MOD, D_MODEL)

    prompt = _stream_set(x_prompt, seq, bp, 0, ffn_tm=min(FFN_ROWS, seq), proj_tm=min(PROJ_ROWS, seq),
                         merge_tm=min(MERGE_ROWS, seq), chunk=min(WKV_CHUNK, seq), pos0=0)
    sample = _stream_set(x_sample, dseq, bs, bp, ffn_tm=bs * dseq, proj_tm=bs * dseq, merge_tm=dseq,
                         chunk=dseq, pos0=PAST_LEN)

    outs = {"p": ([], [], []), "s": ([], [], [])}
    ffn_bf16 = {}
    for l in range(DEPTH):
        for tag, st in (("s", sample), ("p", prompt)):
            nb, t = st["nb"], st["t"]
            if tag == "p":
                shift0 = jnp.zeros((nb, 1, RWKV_COLS), F32)
                s0 = jnp.zeros((nb, N_PAIRS, LANES, LANES), F32)
                pool0 = jnp.zeros((nb, POOL_CARRY, POOL_WIDTH), F32)
            else:
                shift0 = jnp.pad(state_shift[l], ((0, 0), (0, RWKV_COLS - SHIFT_WIDTH))).reshape(nb, 1, RWKV_COLS)
                s0 = _to_pairs(state_wkv[l].astype(F32))
                pool0 = jnp.pad(state_pool[l], ((0, 0), (POOL_CARRY - POOL_BUF, 0), (0, 0)))
            common = dict(layer=l, t=t, b_off=st["b_off"])

            def ffn(x, which, sub, final):
                args = dict(which=which, sub=sub, tm=st["ffn_tm"], final=final, **common)
                if tag == "s":
                    x, *ffn_bf16[l, which] = _ffn(x, mod, norm_g4, ffn_w1, ffn_w3, ffn_w2, fg, emit_bf16=True, **args)
                    return x
                return _ffn(x, mod, norm_g4, *ffn_bf16[l, which], fg, **args)

            x = st["x"]
            x = ffn(x, 0, 0, False)
            proj = _in_proj(x, mod, norm_g4, w_in_p, tm=st["proj_tm"], **common)
            yg, s_new = _wkv(proj, shift0, s0, mu, vecs, w_lr, layer=l, nb=nb, t=t, chunk=st["chunk"])
            x = _merge(x, mod, yg, proj, pool0, w_o_b, w_pool_b, ps, w_out_b, nb=nb,
                       tm=st["merge_tm"], pos0=st["pos0"], **common)
            x = ffn(x, 1, 2, l == DEPTH - 1)
            st["x"] = x
            wkv_l, shift_l, pool_l = outs[tag]
            wkv_l.append(_from_pairs(s_new).astype(dt))
            proj3 = proj.reshape(nb, t, PROJ_WIDTH)
            shift_l.append(proj3[:, -1, :SHIFT_WIDTH])
            pool_l.append(proj3[:, t - POOL_BUF:, POOL_COL:])

    yp = prompt["x"].reshape(bp, seq, D_MODEL)
    ys = sample["x"].reshape(bs, dseq, D_MODEL)
    pw, psh, ppl = (jnp.stack(v) for v in outs["p"])
    sw, ssh, spl = (jnp.stack(v) for v in outs["s"])
    return (yp, ys, pw, psh, ppl, sw, ssh, spl)
```
